```python
import math
import jax, jax.numpy as jnp
from jax import lax
import numpy as np

D_MODEL = 2048
BATCH = 4
SEQ = 4096
DEPTH = 2

N_ATTN_LAYERS = (DEPTH + 1) // 2
N_REC_LAYERS = DEPTH // 2
Q_BLOCK = 128
NEG_INF = -1e30
NORM_EPS = 1e-6

A_HEADS = 8
A_QK_DIM = 64
A_V_DIM = 2 * A_QK_DIM
A_SUBLN_EPS = 1e-5

B_HEADS = 8
B_Q_LORA = 768
B_KV_LORA = 512
B_NOPE_DIM = 128
B_ROPE_DIM = 64
B_V_DIM = 128
B_QK_DIM = B_NOPE_DIM + B_ROPE_DIM
ROPE_THETA = 10000.0

A_Q_COLS = A_HEADS * 2 * A_QK_DIM
A_K_COLS = A_HEADS * 2 * A_QK_DIM
A_V_COLS = A_HEADS * A_V_DIM
ATTN_IN_COLS = A_Q_COLS + A_K_COLS + A_V_COLS + B_Q_LORA + B_KV_LORA + B_ROPE_DIM
MIX_WIDTH = A_HEADS * A_V_DIM + B_HEADS * B_V_DIM

LRU_WIDTH = D_MODEL
LRU_BLOCKS = 8
LRU_BLOCK_W = LRU_WIDTH // LRU_BLOCKS
CONV_WIDTH = 4
LRU_C = 8.0

D_FF = 4 * D_MODEL

kernel_name = "hybrid_diffattn_mla_rglru_block"


def rms_norm(x, g, eps=NORM_EPS):
    xf = x.astype(jnp.float32)
    y = xf * lax.rsqrt(jnp.mean(xf * xf, axis=-1, keepdims=True) + eps)
    return (y * g.astype(jnp.float32)).astype(x.dtype)


def rope_tables(positions):
    inv_freq = 1.0 / (ROPE_THETA ** (jnp.arange(0, B_ROPE_DIM, 2, dtype=jnp.float32) / B_ROPE_DIM))
    ang = positions.astype(jnp.float32)[..., None] * inv_freq
    return jnp.cos(ang)[:, :, None, :], jnp.sin(ang)[:, :, None, :]


def apply_rope(x, cos, sin):
    xf = x.astype(jnp.float32)
    x1, x2 = jnp.split(xf, 2, axis=-1)
    out = jnp.concatenate([x1 * cos - x2 * sin, x2 * cos + x1 * sin], axis=-1)
    return out.astype(x.dtype)


def causal_block_attention(q, k, v, scale):
    B, H, M, S, dk = q.shape
    dv = v.shape[-1]
    nb = S // Q_BLOCK
    qb = jnp.moveaxis(q.reshape(B, H, M, nb, Q_BLOCK, dk), 3, 0)
    kf = k.astype(jnp.float32)
    vf = v.astype(jnp.float32)
    key_pos = jnp.arange(S)

    def one_block(args):
        q_blk, start = args
        s = jnp.einsum('bhmqd,bhmkd->bhmqk', q_blk.astype(jnp.float32), kf) * scale
        q_pos = start + jnp.arange(Q_BLOCK)
        mask = key_pos[None, :] <= q_pos[:, None]
        p = jax.nn.softmax(jnp.where(mask, s, NEG_INF), axis=-1)
        return jnp.einsum('bhmqk,bhkv->bhmqv', p, vf)

    starts = jnp.arange(nb) * Q_BLOCK
    out = lax.map(one_block, (qb, starts))
    out = jnp.moveaxis(out, 0, 3).reshape(B, H, M, S, dv)
    return out.astype(q.dtype)


def attention_mixer(h, cos, sin, layer_idx, w_in, lq1, lk1, lq2, lk2, subln,
                    q_norm, kv_norm, w_uq, w_ukv, w_out):
    B, S, _ = h.shape
    proj = h @ w_in
    o0 = A_Q_COLS
    o1 = o0 + A_K_COLS
    o2 = o1 + A_V_COLS
    o3 = o2 + B_Q_LORA
    o4 = o3 + B_KV_LORA
    qa, ka, va = proj[..., :o0], proj[..., o0:o1], proj[..., o1:o2]
    cq, ckv, kr = proj[..., o2:o3], proj[..., o3:o4], proj[..., o4:]

    qa = jnp.transpose(qa.reshape(B, S, A_HEADS, 2, A_QK_DIM), (0, 2, 3, 1, 4))
    ka = jnp.transpose(ka.reshape(B, S, A_HEADS, 2, A_QK_DIM), (0, 2, 3, 1, 4))
    va = jnp.transpose(va.reshape(B, S, A_HEADS, A_V_DIM), (0, 2, 1, 3))
    oa = causal_block_attention(qa, ka, va, A_QK_DIM ** -0.5)
    lambda_init = 0.8 - 0.6 * math.exp(-0.3 * layer_idx)
    lam = (jnp.exp(jnp.sum(lq1.astype(jnp.float32) * lk1.astype(jnp.float32)))
           - jnp.exp(jnp.sum(lq2.astype(jnp.float32) * lk2.astype(jnp.float32)))
           + lambda_init)
    oa = oa[:, :, 0] - lam.astype(oa.dtype) * oa[:, :, 1]
    oa = rms_norm(oa, subln, A_SUBLN_EPS) * (1.0 - lambda_init)
    oa = jnp.transpose(oa, (0, 2, 1, 3)).reshape(B, S, A_HEADS * A_V_DIM)

    q = (rms_norm(cq, q_norm) @ w_uq).reshape(B, S, B_HEADS, B_QK_DIM)
    q = jnp.concatenate([q[..., :B_NOPE_DIM], apply_rope(q[..., B_NOPE_DIM:], cos, sin)], axis=-1)
    kv = (rms_norm(ckv, kv_norm) @ w_ukv).reshape(B, S, B_HEADS, B_NOPE_DIM + B_V_DIM)
    k_nope, vb = kv[..., :B_NOPE_DIM], kv[..., B_NOPE_DIM:]
    k_rope = apply_rope(kr[:, :, None, :], cos, sin)
    k = jnp.concatenate([k_nope, jnp.broadcast_to(k_rope, (B, S, B_HEADS, B_ROPE_DIM))], axis=-1)
    q = jnp.transpose(q, (0, 2, 1, 3))[:, :, None]
    k = jnp.transpose(k, (0, 2, 1, 3))[:, :, None]
    vb = jnp.transpose(vb, (0, 2, 1, 3))
    ob = causal_block_attention(q, k, vb, B_QK_DIM ** -0.5)[:, :, 0]
    ob = jnp.transpose(ob, (0, 2, 1, 3)).reshape(B, S, B_HEADS * B_V_DIM)

    o = jnp.concatenate([oa.astype(h.dtype), ob.astype(h.dtype)], axis=-1)
    return o @ w_out


def recurrent_mixer(h, w_in, conv_w, conv_b, w_a, b_a, w_x, b_x, lam, w_out):
    B, S, _ = h.shape
    proj = h @ w_in
    y = jax.nn.gelu(proj[..., :LRU_WIDTH], approximate=True)
    xr = proj[..., LRU_WIDTH:]
    xc = lax.conv_general_dilated(
        xr, conv_w[:, None, :], window_strides=(1,), padding=[(CONV_WIDTH - 1, 0)],
        dimension_numbers=('NWC', 'WIO', 'NWC'), feature_group_count=LRU_WIDTH) + conv_b
    xb = xc.reshape(B, S, LRU_BLOCKS, LRU_BLOCK_W)
    r = jax.nn.sigmoid((jnp.einsum('bsni,nio->bsno', xb, w_a) + b_a).astype(jnp.float32))
    i = jax.nn.sigmoid((jnp.einsum('bsni,nio->bsno', xb, w_x) + b_x).astype(jnp.float32))
    r = r.reshape(B, S, LRU_WIDTH)
    i = i.reshape(B, S, LRU_WIDTH)
    log_a = -LRU_C * r * jax.nn.softplus(-lam.astype(jnp.float32))
    a = jnp.exp(log_a)
    b = jnp.sqrt(-jnp.expm1(2.0 * log_a)) * (i * xc.astype(jnp.float32))

    def combine(c1, c2):
        a1, b1 = c1
        a2, b2 = c2
        return a1 * a2, a2 * b1 + b2

    _, hs = lax.associative_scan(combine, (a, b), axis=1)
    return (y * hs.astype(h.dtype)) @ w_out


def squared_relu_mlp(h, w1, w2):
    u = jax.nn.relu(h @ w1)
    return (u * u) @ w2


def setup_inputs(seed: int = 0) -> dict:
    key = jax.random.key(seed)
    ks = iter(jax.random.split(key, 40))
    f32 = jnp.float32

    def dense(shape, fan_in):
        return jax.random.normal(next(ks), shape, f32) * (fan_in ** -0.5)

    def gain(shape):
        return 1.0 + 0.02 * jax.random.normal(next(ks), shape, f32)

    def small(shape, scale=0.01):
        return scale * jax.random.normal(next(ks), shape, f32)

    nA, nR = N_ATTN_LAYERS, N_REC_LAYERS
    x = jax.random.normal(next(ks), (BATCH, SEQ, D_MODEL), f32)
    offsets = jax.random.randint(next(ks), (BATCH, 1), 0, 1024, dtype=jnp.int32)
    positions = offsets + jnp.arange(SEQ, dtype=jnp.int32)[None, :]

    a_c = jax.random.uniform(next(ks), (nR, LRU_WIDTH), f32, 0.81, 0.998)
    s = a_c ** (1.0 / LRU_C)
    rec_lambda = jnp.log(s) - jnp.log1p(-s)

    return {
        "x": x,
        "positions": positions,
        "norm_mix": gain((DEPTH, D_MODEL)),
        "norm_mlp": gain((DEPTH, D_MODEL)),
        "norm_final": gain((D_MODEL,)),
        "attn_w_in": dense((nA, D_MODEL, ATTN_IN_COLS), D_MODEL),
        "attn_lambda_q1": small((nA, A_QK_DIM), 0.1),
        "attn_lambda_k1": small((nA, A_QK_DIM), 0.1),
        "attn_lambda_q2": small((nA, A_QK_DIM), 0.1),
        "attn_lambda_k2": small((nA, A_QK_DIM), 0.1),
        "attn_subln": gain((nA, A_V_DIM)),
        "attn_q_norm": gain((nA, B_Q_LORA)),
        "attn_kv_norm": gain((nA, B_KV_LORA)),
        "attn_w_uq": dense((nA, B_Q_LORA, B_HEADS * B_QK_DIM), B_Q_LORA),
        "attn_w_ukv": dense((nA, B_KV_LORA, B_HEADS * (B_NOPE_DIM + B_V_DIM)), B_KV_LORA),
        "attn_w_out": dense((nA, MIX_WIDTH, D_MODEL), MIX_WIDTH),
        "rec_w_in": dense((nR, D_MODEL, 2 * LRU_WIDTH), D_MODEL),
        "rec_conv_w": dense((nR, CONV_WIDTH, LRU_WIDTH), CONV_WIDTH),
        "rec_conv_b": small((nR, LRU_WIDTH)),
        "rec_w_a": dense((nR, LRU_BLOCKS, LRU_BLOCK_W, LRU_BLOCK_W), LRU_BLOCK_W),
        "rec_b_a": small((nR, LRU_BLOCKS, LRU_BLOCK_W)),
        "rec_w_x": dense((nR, LRU_BLOCKS, LRU_BLOCK_W, LRU_BLOCK_W), LRU_BLOCK_W),
        "rec_b_x": small((nR, LRU_BLOCKS, LRU_BLOCK_W)),
        "rec_lambda": rec_lambda,
        "rec_w_out": dense((nR, LRU_WIDTH, D_MODEL), LRU_WIDTH),
        "mlp_w1": dense((DEPTH, D_MODEL, D_FF), D_MODEL),
        "mlp_w2": dense((DEPTH, D_FF, D_MODEL), D_FF),
    }


def reference(x, positions, norm_mix, norm_mlp, norm_final,
              attn_w_in, attn_lambda_q1, attn_lambda_k1, attn_lambda_q2, attn_lambda_k2,
              attn_subln, attn_q_norm, attn_kv_norm, attn_w_uq, attn_w_ukv, attn_w_out,
              rec_w_in, rec_conv_w, rec_conv_b, rec_w_a, rec_b_a, rec_w_x, rec_b_x,
              rec_lambda, rec_w_out, mlp_w1, mlp_w2):
    cos, sin = rope_tables(positions)
    h = x
    for layer in range(DEPTH):
        j = layer // 2
        hn = rms_norm(h, norm_mix[layer])
        if layer % 2 == 0:
            mix = attention_mixer(hn, cos, sin, layer, attn_w_in[j],
                                  attn_lambda_q1[j], attn_lambda_k1[j],
                                  attn_lambda_q2[j], attn_lambda_k2[j], attn_subln[j],
                                  attn_q_norm[j], attn_kv_norm[j], attn_w_uq[j],
                                  attn_w_ukv[j], attn_w_out[j])
        else:
            mix = recurrent_mixer(hn, rec_w_in[j], rec_conv_w[j], rec_conv_b[j],
                                  rec_w_a[j], rec_b_a[j], rec_w_x[j], rec_b_x[j],
                                  rec_lambda[j], rec_w_out[j])
        h = h + mix.astype(h.dtype)
        h = h + squared_relu_mlp(rms_norm(h, norm_mlp[layer]), mlp_w1[layer], mlp_w2[layer]).astype(h.dtype)
    return rms_norm(h, norm_final)
```

```python
import functools
import math

import jax
import jax.numpy as jnp
from jax import lax
from jax.experimental import pallas as pl
from jax.experimental.pallas import tpu as pltpu

F32 = jnp.float32
BF16 = jnp.bfloat16

NORM_EPS = 1e-6
NEG_INF = -1e30
A_HEADS = 8
A_QK_DIM = 64
A_V_DIM = 128
A_SUBLN_EPS = 1e-5
B_HEADS = 8
B_Q_LORA = 768
B_KV_LORA = 512
B_NOPE_DIM = 128
B_ROPE_DIM = 64
B_V_DIM = 128
B_QK_DIM = B_NOPE_DIM + B_ROPE_DIM
ROPE_THETA = 10000.0
LRU_BLOCKS = 8
CONV_WIDTH = 4
LRU_C = 8.0

LANES = 128
SUBLANES = 8
VMEM_LIMIT = 48 * 1024 * 1024

C_CQ = 0
C_CKV = C_CQ + B_Q_LORA
C_KR = C_CKV + B_KV_LORA
C_QA = C_KR + LANES
C_KA = C_QA + A_HEADS * 2 * A_QK_DIM
C_VA = C_KA + A_HEADS * 2 * A_QK_DIM
C_END = C_VA + A_HEADS * A_V_DIM


def _params(*sem):
    return pltpu.CompilerParams(dimension_semantics=sem, vmem_limit_bytes=VMEM_LIMIT)


def _tile(n, pref):
    t = min(n, pref)
    assert n % t == 0, (n, t)
    return t


def _rms(x, g, eps):
    return x * lax.rsqrt(jnp.mean(x * x, axis=-1, keepdims=True) + eps) * g


def _rope_table_kernel(pos_ref, invf_ref, o_ref):
    ang = pos_ref[...] * invf_ref[...]
    lane = lax.broadcasted_iota(jnp.int32, ang.shape, 1)
    o_ref[...] = jnp.where(lane < B_ROPE_DIM, jnp.cos(ang), jnp.sin(ang))


def _rope_table(pos_lanes, invf):
    t = pos_lanes.shape[0]
    tm = _tile(t, 1024)
    return pl.pallas_call(
        _rope_table_kernel,
        out_shape=jax.ShapeDtypeStruct((t, LANES), F32),
        grid=(t // tm,),
        in_specs=[pl.BlockSpec((tm, LANES), lambda i: (i, 0)),
                  pl.BlockSpec((1, LANES), lambda i: (0, 0))],
        out_specs=pl.BlockSpec((tm, LANES), lambda i: (i, 0)),
        compiler_params=_params("parallel"),
        name="rope_table",
    )(pos_lanes, invf)


def _norm_matmul_kernel(x_ref, g_ref, w_ref, o_ref, xn_ref):
    @pl.when(pl.program_id(1) == 0)
    def _():
        xn_ref[...] = _rms(x_ref[...], g_ref[...], NORM_EPS).astype(BF16)

    o_ref[...] = jnp.dot(xn_ref[...], w_ref[...],
                         preferred_element_type=F32).astype(o_ref.dtype)


def _norm_matmul(x, g, w, out_dtype, name, tm=512, tn=640):
    t, k = x.shape
    n = w.shape[1]
    tm = _tile(t, tm)
    tn = _tile(n, tn)
    return pl.pallas_call(
        _norm_matmul_kernel,
        out_shape=jax.ShapeDtypeStruct((t, n), out_dtype),
        grid=(t // tm, n // tn),
        in_specs=[pl.BlockSpec((tm, k), lambda i, j: (i, 0)),
                  pl.BlockSpec((1, k), lambda i, j: (0, 0)),
                  pl.BlockSpec((k, tn), lambda i, j: (0, j))],
        out_specs=pl.BlockSpec((tm, tn), lambda i, j: (i, j)),
        scratch_shapes=[pltpu.VMEM((tm, k), BF16)],
        compiler_params=_params("parallel", "arbitrary"),
        name=name,
    )(x, g, w)


def _matmul_res_kernel(*refs, n_lhs):
    lhs = refs[:n_lhs]
    ws = refs[n_lhs:2 * n_lhs]
    res_ref, o_ref = refs[2 * n_lhs], refs[2 * n_lhs + 1]
    acc = res_ref[...]
    for a, w in zip(lhs, ws):
        acc = acc + jnp.dot(a[...], w[...], preferred_element_type=F32)
    o_ref[...] = acc


def _matmul_res(lhs_list, w_list, res, name, tm=512, tn=512):
    t, n = res.shape
    tm = _tile(t, tm)
    tn = _tile(n, tn)
    n_lhs = len(lhs_list)
    in_specs = []
    for a in lhs_list:
        in_specs.append(pl.BlockSpec((tm, a.shape[1]), lambda i, j: (i, 0)))
    for w in w_list:
        in_specs.append(pl.BlockSpec((w.shape[0], tn), lambda i, j: (0, j)))
    in_specs.append(pl.BlockSpec((tm, tn), lambda i, j: (i, j)))
    return pl.pallas_call(
        functools.partial(_matmul_res_kernel, n_lhs=n_lhs),
        out_shape=jax.ShapeDtypeStruct((t, n), F32),
        grid=(t // tm, n // tn),
        in_specs=in_specs,
        out_specs=pl.BlockSpec((tm, tn), lambda i, j: (i, j)),
        compiler_params=_params("parallel", "parallel"),
        name=name,
    )(*lhs_list, *w_list, res)


def _mlp_kernel(h_ref, g_ref, w1_ref, w2_ref, gf_ref, o_ref, xn_ref, *, final_norm):
    j = pl.program_id(1)

    @pl.when(j == 0)
    def _():
        h = h_ref[...]
        xn_ref[...] = _rms(h, g_ref[...], NORM_EPS).astype(BF16)
        o_ref[...] = h

    u = jnp.maximum(jnp.dot(xn_ref[...], w1_ref[...], preferred_element_type=F32), 0.0)
    u = (u * u).astype(BF16)
    o_ref[...] += jnp.dot(u, w2_ref[...], preferred_element_type=F32)

    if final_norm:
        @pl.when(j == pl.num_programs(1) - 1)
        def _():
            o_ref[...] = _rms(o_ref[...], gf_ref[...], NORM_EPS)


def _mlp(h, g, w1, w2, gf, final_norm, name, tm=512, tf=512):
    t, d = h.shape
    f = w1.shape[1]
    tm = _tile(t, tm)
    tf = _tile(f, tf)
    return pl.pallas_call(
        functools.partial(_mlp_kernel, final_norm=final_norm),
        out_shape=jax.ShapeDtypeStruct((t, d), F32),
        grid=(t // tm, f // tf),
        in_specs=[pl.BlockSpec((tm, d), lambda i, j: (i, 0)),
                  pl.BlockSpec((1, d), lambda i, j: (0, 0)),
                  pl.BlockSpec((d, tf), lambda i, j: (0, j)),
                  pl.BlockSpec((tf, d), lambda i, j: (j, 0)),
                  pl.BlockSpec((1, d), lambda i, j: (0, 0))],
        out_specs=pl.BlockSpec((tm, d), lambda i, j: (i, 0)),
        scratch_shapes=[pltpu.VMEM((tm, d), BF16)],
        compiler_params=_params("parallel", "arbitrary"),
        name=name,
    )(h, g, w1, w2, gf)


def _mla_prep_kernel(lat_ref, kr_ref, cs_ref, gq_ref, gkv_ref, wq_ref, wkv_ref,
                     q_ref, kv_ref):
    lat = lat_ref[...].astype(F32)
    cs = cs_ref[...]
    scale = B_QK_DIM ** -0.5
    cqn = _rms(lat[:, :B_Q_LORA], gq_ref[...], NORM_EPS).astype(BF16)
    q = jnp.dot(cqn, wq_ref[...], preferred_element_type=F32)
    cs_scaled = cs * scale
    for h in range(B_HEADS):
        lo = h * 2 * LANES
        q_ref[:, lo:lo + LANES] = (q[:, lo:lo + LANES] * scale).astype(BF16)
        q_ref[:, lo + LANES:lo + 2 * LANES] = (
            q[:, lo + LANES:lo + 2 * LANES] * cs_scaled).astype(BF16)
    ckvn = _rms(lat[:, B_Q_LORA:], gkv_ref[...], NORM_EPS).astype(BF16)
    kv = jnp.dot(ckvn, wkv_ref[...], preferred_element_type=F32)
    nk = B_HEADS * B_NOPE_DIM
    kv_ref[:, :nk] = kv[:, :nk].astype(BF16)
    kv_ref[:, nk + LANES:] = kv[:, nk:].astype(BF16)
    t = kr_ref[...].astype(F32) * cs
    kv_ref[:, nk:nk + LANES] = (t + pltpu.roll(t, B_ROPE_DIM, 1)).astype(BF16)


def _mla_prep(proj, cs, gq, gkv, wq, wkv, tm=512):
    t = proj.shape[0]
    tm = _tile(t, tm)
    lat_w = B_Q_LORA + B_KV_LORA
    nq = B_HEADS * 2 * LANES
    nkv = B_HEADS * (B_NOPE_DIM + B_V_DIM) + LANES
    return pl.pallas_call(
        _mla_prep_kernel,
        out_shape=(jax.ShapeDtypeStruct((t, nq), BF16),
                   jax.ShapeDtypeStruct((t, nkv), BF16)),
        grid=(t // tm,),
        in_specs=[pl.BlockSpec((tm, lat_w), lambda i: (i, 0)),
                  pl.BlockSpec((tm, LANES), lambda i: (i, C_KR // LANES)),
                  pl.BlockSpec((tm, LANES), lambda i: (i, 0)),
                  pl.BlockSpec((1, B_Q_LORA), lambda i: (0, 0)),
                  pl.BlockSpec((1, B_KV_LORA), lambda i: (0, 0)),
                  pl.BlockSpec(wq.shape, lambda i: (0, 0)),
                  pl.BlockSpec(wkv.shape, lambda i: (0, 0))],
        out_specs=(pl.BlockSpec((tm, nq), lambda i: (i, 0)),
                   pl.BlockSpec((tm, nkv), lambda i: (i, 0))),
        compiler_params=_params("parallel"),
        name="mla_prep",
    )(proj, proj, cs, gq, gkv, wq, wkv)


def _flash_sweep(q, k_at, v_at, qi, tq, dv):
    rows = q.shape[0]

    def step(kb, carry, masked):
        m, l, acc = carry
        s = lax.dot_general(q, k_at(kb), (((1,), (1,)), ((), ())),
                            preferred_element_type=F32)
        if masked:
            row = lax.broadcasted_iota(jnp.int32, s.shape, 0) & (tq - 1)
            col = lax.broadcasted_iota(jnp.int32, s.shape, 1)
            s = jnp.where(col <= row, s, NEG_INF)
        m_new = jnp.maximum(m, jnp.max(s, axis=-1, keepdims=True))
        alpha = jnp.exp(m - m_new)
        p = jnp.exp(s - m_new)
        l = alpha * l + jnp.sum(p, axis=-1, keepdims=True)
        acc = alpha * acc + jnp.dot(p.astype(BF16), v_at(kb),
                                    preferred_element_type=F32)
        return m_new, l, acc

    init = (jnp.full((rows, 1), NEG_INF, F32), jnp.zeros((rows, 1), F32),
            jnp.zeros((rows, dv), F32))
    carry = lax.fori_loop(0, qi, lambda kb, c: step(kb, c, False), init)
    _, l, acc = step(qi, carry, True)
    return acc, l


def _diff_attn_kernel(q_ref, k_ref, v_ref, lam_ref, g_ref, o_ref, *, tq, lambda_init):
    qi = pl.program_id(2)
    q = q_ref[...]
    lane = lax.broadcasted_iota(jnp.int32, q.shape, 1)
    zero = jnp.zeros_like(q)
    q2 = jnp.concatenate([jnp.where(lane < A_QK_DIM, q, zero),
                          jnp.where(lane < A_QK_DIM, zero, q)], axis=0)

    def k_at(kb):
        return k_ref[pl.ds(pl.multiple_of(kb * tq, tq), tq), :]

    def v_at(kb):
        return v_ref[pl.ds(pl.multiple_of(kb * tq, tq), tq), :]

    acc, l = _flash_sweep(q2, k_at, v_at, qi, tq, A_V_DIM)
    o = acc / l
    lv = lam_ref[...]
    lam = (jnp.exp(jnp.sum(lv[0:1] * lv[1:2], axis=-1, keepdims=True))
           - jnp.exp(jnp.sum(lv[2:3] * lv[3:4], axis=-1, keepdims=True))
           + lambda_init)
    d = o[:tq] - lam * o[tq:]
    o_ref[...] = (_rms(d, g_ref[...], A_SUBLN_EPS) * (1.0 - lambda_init)).astype(o_ref.dtype)


def _diff_attn(proj, lamv, subln, batch, seq, lambda_init, tq=256):
    t = proj.shape[0]
    tq = _tile(seq, tq)
    nq = seq // tq
    qc, kc, vc = C_QA // LANES, C_KA // LANES, C_VA // LANES
    return pl.pallas_call(
        functools.partial(_diff_attn_kernel, tq=tq, lambda_init=lambda_init),
        out_shape=jax.ShapeDtypeStruct((t, A_HEADS * A_V_DIM), BF16),
        grid=(batch, A_HEADS, nq),
        in_specs=[pl.BlockSpec((tq, LANES), lambda b, h, i: (b * nq + i, qc + h)),
                  pl.BlockSpec((seq, LANES), lambda b, h, i: (b, kc + h)),
                  pl.BlockSpec((seq, LANES), lambda b, h, i: (b, vc + h)),
                  pl.BlockSpec(lamv.shape, lambda b, h, i: (0, 0)),
                  pl.BlockSpec((1, A_V_DIM), lambda b, h, i: (0, 0))],
        out_specs=pl.BlockSpec((tq, LANES), lambda b, h, i: (b * nq + i, h)),
        compiler_params=_params("parallel", "parallel", "arbitrary"),
        name="diff_attn",
    )(proj, proj, proj, lamv, subln)


def _mla_attn_kernel(q_ref, kn_ref, kr_ref, v_ref, o_ref, *, tq):
    qi = pl.program_id(2)

    def k_at(kb):
        sl = pl.ds(pl.multiple_of(kb * tq, tq), tq)
        return jnp.concatenate([kn_ref[sl, :], kr_ref[sl, :]], axis=1)

    def v_at(kb):
        return v_ref[pl.ds(pl.multiple_of(kb * tq, tq), tq), :]

    acc, l = _flash_sweep(q_ref[...], k_at, v_at, qi, tq, B_V_DIM)
    o_ref[...] = (acc / l).astype(o_ref.dtype)


def _mla_attn(qb, kvb, batch, seq, tq=512):
    t = qb.shape[0]
    tq = _tile(seq, tq)
    nq = seq // tq
    return pl.pallas_call(
        functools.partial(_mla_attn_kernel, tq=tq),
        out_shape=jax.ShapeDtypeStruct((t, B_HEADS * B_V_DIM), BF16),
        grid=(batch, B_HEADS, nq),
        in_specs=[pl.BlockSpec((tq, 2 * LANES), lambda b, h, i: (b * nq + i, h)),
                  pl.BlockSpec((seq, LANES), lambda b, h, i: (b, h)),
                  pl.BlockSpec((seq, LANES), lambda b, h, i: (b, B_HEADS)),
                  pl.BlockSpec((seq, LANES), lambda b, h, i: (b, B_HEADS + 1 + h))],
        out_specs=pl.BlockSpec((tq, LANES), lambda b, h, i: (b * nq + i, h)),
        compiler_params=_params("parallel", "parallel", "arbitrary"),
        name="mla_attn",
    )(qb, kvb, kvb, kvb)


def _rglru_kernel(y_ref, x_ref, cw_ref, cb_ref, wg_ref, bg_ref, lam_ref, o_ref,
                  halo_ref, h_ref, *, ts, bw):
    @pl.when(pl.program_id(2) == 0)
    def _():
        halo_ref[...] = jnp.zeros_like(halo_ref)
        h_ref[...] = jnp.zeros_like(h_ref)

    x = x_ref[...].astype(F32)
    ext = jnp.concatenate([halo_ref[...], x], axis=0)
    halo_ref[...] = x[ts - SUBLANES:]
    cw = cw_ref[...]
    xc = cb_ref[...] + cw[CONV_WIDTH - 1:CONV_WIDTH] * x
    for k in range(1, CONV_WIDTH):
        w = cw[CONV_WIDTH - 1 - k:CONV_WIDTH - k]
        xc = xc + w * pltpu.roll(ext, k, 0)[SUBLANES:]

    gates = jnp.dot(xc.astype(BF16), wg_ref[...], preferred_element_type=F32) + bg_ref[...]
    r = jax.nn.sigmoid(gates[:, :bw])
    i = jax.nn.sigmoid(gates[:, bw:])
    log_a = -LRU_C * r * jax.nn.softplus(-lam_ref[...])
    a = jnp.exp(log_a)
    b = jnp.sqrt(-jnp.tanh(log_a) * (a * a + 1.0)) * (i * xc)

    ng = ts // SUBLANES
    a3 = a.reshape(ng, SUBLANES, bw)
    b3 = b.reshape(ng, SUBLANES, bw)
    sub = lax.broadcasted_iota(jnp.int32, a3.shape, 1)
    for k in (1, 2, 4):
        keep = sub >= k
        b3 = jnp.where(keep, a3 * pltpu.roll(b3, k, 1) + b3, b3)
        a3 = jnp.where(keep, a3 * pltpu.roll(a3, k, 1), a3)

    h = h_ref[...]
    rows = []
    for g in range(ng):
        hv = a3[g] * h + b3[g]
        rows.append(hv)
        h = hv[SUBLANES - 1:SUBLANES]
    h_ref[...] = h
    hs = jnp.concatenate(rows, axis=0)
    y = jax.nn.gelu(y_ref[...].astype(F32), approximate=True)
    o_ref[...] = (y * hs).astype(o_ref.dtype)


def _rglru(proj, conv_w, conv_b, wg, bg, lam, batch, seq, ts=256):
    t = proj.shape[0]
    width = proj.shape[1] // 2
    bw = width // LRU_BLOCKS
    ts = _tile(seq, ts)
    ns = seq // ts
    return pl.pallas_call(
        functools.partial(_rglru_kernel, ts=ts, bw=bw),
        out_shape=jax.ShapeDtypeStruct((t, width), BF16),
        grid=(batch, LRU_BLOCKS, ns),
        in_specs=[pl.BlockSpec((ts, bw), lambda b, n, s: (b * ns + s, n)),
                  pl.BlockSpec((ts, bw), lambda b, n, s: (b * ns + s, LRU_BLOCKS + n)),
                  pl.BlockSpec((CONV_WIDTH, bw), lambda b, n, s: (0, n)),
                  pl.BlockSpec((1, bw), lambda b, n, s: (0, n)),
                  pl.BlockSpec((None, bw, 2 * bw), lambda b, n, s: (n, 0, 0)),
                  pl.BlockSpec((None, 1, 2 * bw), lambda b, n, s: (n, 0, 0)),
                  pl.BlockSpec((1, bw), lambda b, n, s: (0, n))],
        out_specs=pl.BlockSpec((ts, bw), lambda b, n, s: (b * ns + s, n)),
        scratch_shapes=[pltpu.VMEM((SUBLANES, bw), F32), pltpu.VMEM((1, bw), F32)],
        compiler_params=_params("parallel", "parallel", "arbitrary"),
        name="rglru",
    )(proj, proj, conv_w, conv_b, wg, bg, lam)


def _rot_half_cols(w):
    half = w.shape[-1] // 2
    return jnp.concatenate([-w[..., half:], w[..., :half]], axis=-1)


def _prep_attn_w_in(w):
    o0 = A_HEADS * 2 * A_QK_DIM
    o1 = 2 * o0
    o2 = o1 + A_HEADS * A_V_DIM
    o3 = o2 + B_Q_LORA
    o4 = o3 + B_KV_LORA
    kr = w[:, o4:]
    return jnp.concatenate(
        [w[:, o2:o3], w[:, o3:o4], kr, _rot_half_cols(kr),
         w[:, :o0] * (A_QK_DIM ** -0.5), w[:, o0:o1], w[:, o1:o2]], axis=1).astype(BF16)


def _prep_w_uq(w):
    k = w.shape[0]
    w = w.reshape(k, B_HEADS, B_QK_DIM)
    rope = w[:, :, B_NOPE_DIM:]
    return jnp.concatenate([w[:, :, :B_NOPE_DIM], rope, _rot_half_cols(rope)],
                           axis=-1).reshape(k, B_HEADS * 2 * LANES).astype(BF16)


def _prep_w_ukv(w):
    k = w.shape[0]
    w = w.reshape(k, B_HEADS, B_NOPE_DIM + B_V_DIM)
    return jnp.concatenate([w[:, :, :B_NOPE_DIM].reshape(k, -1),
                            w[:, :, B_NOPE_DIM:].reshape(k, -1)], axis=1).astype(BF16)


def kernel(x, positions, norm_mix, norm_mlp, norm_final, attn_w_in, attn_lambda_q1,
           attn_lambda_k1, attn_lambda_q2, attn_lambda_k2, attn_subln, attn_q_norm,
           attn_kv_norm, attn_w_uq, attn_w_ukv, attn_w_out, rec_w_in, rec_conv_w,
           rec_conv_b, rec_w_a, rec_b_a, rec_w_x, rec_b_x, rec_lambda, rec_w_out,
           mlp_w1, mlp_w2):
    batch, seq, d = x.shape
    t = batch * seq
    h0 = x.reshape(t, d)
    row = lambda v: v.reshape(1, -1)

    half = B_ROPE_DIM // 2
    inv_freq = 1.0 / (ROPE_THETA ** (jnp.arange(0, B_ROPE_DIM, 2, dtype=F32) / B_ROPE_DIM))
    invf = jnp.tile(inv_freq, LANES // half).reshape(1, LANES)
    pos_lanes = jnp.broadcast_to(positions.astype(F32).reshape(t, 1), (t, LANES))
    cs = _rope_table(pos_lanes, invf)

    proj = _norm_matmul(h0, row(norm_mix[0]), _prep_attn_w_in(attn_w_in[0]), BF16,
                        "attn_in_proj")
    qb, kvb = _mla_prep(proj, cs, row(attn_q_norm[0]), row(attn_kv_norm[0]),
                        _prep_w_uq(attn_w_uq[0]), _prep_w_ukv(attn_w_ukv[0]))
    lambda_init = 0.8 - 0.6 * math.exp(-0.3 * 0)
    lamv = jnp.stack([attn_lambda_q1[0], attn_lambda_k1[0],
                      attn_lambda_q2[0], attn_lambda_k2[0]]).astype(F32)
    oa = _diff_attn(proj, lamv, row(attn_subln[0]), batch, seq, lambda_init)
    ob = _mla_attn(qb, kvb, batch, seq)
    w_out = attn_w_out[0].astype(BF16)
    na = A_HEADS * A_V_DIM
    h1 = _matmul_res([oa, ob], [w_out[:na], w_out[na:]], h0, "attn_out_proj")
    h2 = _mlp(h1, row(norm_mlp[0]), mlp_w1[0].astype(BF16), mlp_w2[0].astype(BF16),
              row(norm_final), False, "mlp0")

    proj2 = _norm_matmul(h2, row(norm_mix[1]), rec_w_in[0].astype(BF16), BF16,
                         "rec_in_proj", tn=512)
    wg = jnp.concatenate([rec_w_a[0], rec_w_x[0]], axis=-1).astype(BF16)
    bg = jnp.concatenate([rec_b_a[0], rec_b_x[0]], axis=-1)[:, None, :]
    g = _rglru(proj2, rec_conv_w[0], row(rec_conv_b[0]), wg, bg, row(rec_lambda[0]),
               batch, seq)
    h3 = _matmul_res([g], [rec_w_out[0].astype(BF16)], h2, "rec_out_proj")
    out = _mlp(h3, row(norm_mlp[1]), mlp_w1[1].astype(BF16), mlp_w2[1].astype(BF16),
               row(norm_final), True, "mlp1")
    return out.reshape(batch, seq, d)
```

```python
import functools
import math

import jax
import jax.numpy as jnp
from jax import lax
from jax.experimental import pallas as pl
from jax.experimental.pallas import tpu as pltpu

F32 = jnp.float32
BF16 = jnp.bfloat16

NORM_EPS = 1e-6
NEG_INF = -1e30
A_HEADS = 8
A_QK_DIM = 64
A_V_DIM = 128
A_SUBLN_EPS = 1e-5
B_HEADS = 8
B_Q_LORA = 768
B_KV_LORA = 512
B_NOPE_DIM = 128
B_ROPE_DIM = 64
B_V_DIM = 128
B_QK_DIM = B_NOPE_DIM + B_ROPE_DIM
ROPE_THETA = 10000.0
LRU_BLOCKS = 8
CONV_WIDTH = 4
LRU_C = 8.0
LOG2E = math.log2(math.e)

LANES = 128
SUBLANES = 8
VMEM_LIMIT = 48 * 1024 * 1024

C_CQ = 0
C_CKV = C_CQ + B_Q_LORA
C_KR = C_CKV + B_KV_LORA
C_QA = C_KR + LANES
C_KA = C_QA + A_HEADS * 2 * A_QK_DIM
C_VA = C_KA + A_HEADS * 2 * A_QK_DIM
C_END = C_VA + A_HEADS * A_V_DIM


def _params(*sem):
    return pltpu.CompilerParams(dimension_semantics=sem, vmem_limit_bytes=VMEM_LIMIT)


def _tile(n, pref):
    t = min(n, pref)
    assert n % t == 0, (n, t)
    return t


def _rms(x, g, eps):
    return x * lax.rsqrt(jnp.mean(x * x, axis=-1, keepdims=True) + eps) * g


def _rope_table_kernel(pos_ref, invf_ref, o_ref):
    ang = pos_ref[...] * invf_ref[...]
    lane = lax.broadcasted_iota(jnp.int32, ang.shape, 1)
    o_ref[...] = jnp.where(lane < B_ROPE_DIM, jnp.cos(ang), jnp.sin(ang))


def _rope_table(pos_lanes, invf):
    t = pos_lanes.shape[0]
    tm = _tile(t, 1024)
    return pl.pallas_call(
        _rope_table_kernel,
        out_shape=jax.ShapeDtypeStruct((t, LANES), F32),
        grid=(t // tm,),
        in_specs=[pl.BlockSpec((tm, LANES), lambda i: (i, 0)),
                  pl.BlockSpec((1, LANES), lambda i: (0, 0))],
        out_specs=pl.BlockSpec((tm, LANES), lambda i: (i, 0)),
        compiler_params=_params("parallel"),
        name="rope_table",
    )(pos_lanes, invf)


def _norm_matmul_kernel(x_ref, g_ref, w_ref, o_ref, xn_ref):
    @pl.when(pl.program_id(1) == 0)
    def _():
        xn_ref[...] = _rms(x_ref[...], g_ref[...], NORM_EPS).astype(BF16)

    o_ref[...] = jnp.dot(xn_ref[...], w_ref[...],
                         preferred_element_type=F32).astype(o_ref.dtype)


def _norm_matmul(x, g, w, out_dtype, name, tm=512, tn=640):
    t, k = x.shape
    n = w.shape[1]
    tm = _tile(t, tm)
    tn = _tile(n, tn)
    return pl.pallas_call(
        _norm_matmul_kernel,
        out_shape=jax.ShapeDtypeStruct((t, n), out_dtype),
        grid=(t // tm, n // tn),
        in_specs=[pl.BlockSpec((tm, k), lambda i, j: (i, 0)),
                  pl.BlockSpec((1, k), lambda i, j: (0, 0)),
                  pl.BlockSpec((k, tn), lambda i, j: (0, j))],
        out_specs=pl.BlockSpec((tm, tn), lambda i, j: (i, j)),
        scratch_shapes=[pltpu.VMEM((tm, k), BF16)],
        compiler_params=_params("parallel", "arbitrary"),
        name=name,
    )(x, g, w)


def _matmul_res_kernel(*refs, n_lhs):
    lhs = refs[:n_lhs]
    ws = refs[n_lhs:2 * n_lhs]
    res_ref, o_ref = refs[2 * n_lhs], refs[2 * n_lhs + 1]
    acc = res_ref[...]
    for a, w in zip(lhs, ws):
        acc = acc + jnp.dot(a[...], w[...], preferred_element_type=F32)
    o_ref[...] = acc


def _matmul_res(lhs_list, w_list, res, name, tm=512, tn=512):
    t, n = res.shape
    tm = _tile(t, tm)
    tn = _tile(n, tn)
    n_lhs = len(lhs_list)
    in_specs = []
    for a in lhs_list:
        in_specs.append(pl.BlockSpec((tm, a.shape[1]), lambda i, j: (i, 0)))
    for w in w_list:
        in_specs.append(pl.BlockSpec((w.shape[0], tn), lambda i, j: (0, j)))
    in_specs.append(pl.BlockSpec((tm, tn), lambda i, j: (i, j)))
    return pl.pallas_call(
        functools.partial(_matmul_res_kernel, n_lhs=n_lhs),
        out_shape=jax.ShapeDtypeStruct((t, n), F32),
        grid=(t // tm, n // tn),
        in_specs=in_specs,
        out_specs=pl.BlockSpec((tm, tn), lambda i, j: (i, j)),
        compiler_params=_params("parallel", "parallel"),
        name=name,
    )(*lhs_list, *w_list, res)


def _mlp_kernel(h_ref, g_ref, w1_ref, w2_ref, gf_ref, o_ref, xn_ref, *, final_norm):
    j = pl.program_id(1)

    @pl.when(j == 0)
    def _():
        h = h_ref[...]
        xn_ref[...] = _rms(h, g_ref[...], NORM_EPS).astype(BF16)
        o_ref[...] = h

    u = jnp.maximum(jnp.dot(xn_ref[...], w1_ref[...], preferred_element_type=F32), 0.0)
    u = (u * u).astype(BF16)
    o_ref[...] += jnp.dot(u, w2_ref[...], preferred_element_type=F32)

    if final_norm:
        @pl.when(j == pl.num_programs(1) - 1)
        def _():
            o_ref[...] = _rms(o_ref[...], gf_ref[...], NORM_EPS)


def _mlp(h, g, w1, w2, gf, final_norm, name, tm=512, tf=512):
    t, d = h.shape
    f = w1.shape[1]
    tm = _tile(t, tm)
    tf = _tile(f, tf)
    return pl.pallas_call(
        functools.partial(_mlp_kernel, final_norm=final_norm),
        out_shape=jax.ShapeDtypeStruct((t, d), F32),
        grid=(t // tm, f // tf),
        in_specs=[pl.BlockSpec((tm, d), lambda i, j: (i, 0)),
                  pl.BlockSpec((1, d), lambda i, j: (0, 0)),
                  pl.BlockSpec((d, tf), lambda i, j: (0, j)),
                  pl.BlockSpec((tf, d), lambda i, j: (j, 0)),
                  pl.BlockSpec((1, d), lambda i, j: (0, 0))],
        out_specs=pl.BlockSpec((tm, d), lambda i, j: (i, 0)),
        scratch_shapes=[pltpu.VMEM((tm, d), BF16)],
        compiler_params=_params("parallel", "arbitrary"),
        name=name,
    )(h, g, w1, w2, gf)


def _mla_prep_kernel(lat_ref, kr_ref, cs_ref, gq_ref, gkv_ref, wq_ref, wkv_ref,
                     q_ref, kv_ref):
    lat = lat_ref[...].astype(F32)
    cs = cs_ref[...]
    scale = B_QK_DIM ** -0.5 * LOG2E
    cqn = _rms(lat[:, :B_Q_LORA], gq_ref[...], NORM_EPS).astype(BF16)
    q = jnp.dot(cqn, wq_ref[...], preferred_element_type=F32)
    cs_scaled = cs * scale
    for h in range(B_HEADS):
        lo = h * 2 * LANES
        q_ref[:, lo:lo + LANES] = (q[:, lo:lo + LANES] * scale).astype(BF16)
        q_ref[:, lo + LANES:lo + 2 * LANES] = (
            q[:, lo + LANES:lo + 2 * LANES] * cs_scaled).astype(BF16)
    ckvn = _rms(lat[:, B_Q_LORA:], gkv_ref[...], NORM_EPS).astype(BF16)
    kv = jnp.dot(ckvn, wkv_ref[...], preferred_element_type=F32)
    nk = B_HEADS * B_NOPE_DIM
    kv_ref[:, :nk] = kv[:, :nk].astype(BF16)
    kv_ref[:, nk + LANES:] = kv[:, nk:].astype(BF16)
    t = kr_ref[...].astype(F32) * cs
    kv_ref[:, nk:nk + LANES] = (t + pltpu.roll(t, B_ROPE_DIM, 1)).astype(BF16)


def _mla_prep(proj, cs, gq, gkv, wq, wkv, tm=512):
    t = proj.shape[0]
    tm = _tile(t, tm)
    lat_w = B_Q_LORA + B_KV_LORA
    nq = B_HEADS * 2 * LANES
    nkv = B_HEADS * (B_NOPE_DIM + B_V_DIM) + LANES
    return pl.pallas_call(
        _mla_prep_kernel,
        out_shape=(jax.ShapeDtypeStruct((t, nq), BF16),
                   jax.ShapeDtypeStruct((t, nkv), BF16)),
        grid=(t // tm,),
        in_specs=[pl.BlockSpec((tm, lat_w), lambda i: (i, 0)),
                  pl.BlockSpec((tm, LANES), lambda i: (i, C_KR // LANES)),
                  pl.BlockSpec((tm, LANES), lambda i: (i, 0)),
                  pl.BlockSpec((1, B_Q_LORA), lambda i: (0, 0)),
                  pl.BlockSpec((1, B_KV_LORA), lambda i: (0, 0)),
                  pl.BlockSpec(wq.shape, lambda i: (0, 0)),
                  pl.BlockSpec(wkv.shape, lambda i: (0, 0))],
        out_specs=(pl.BlockSpec((tm, nq), lambda i: (i, 0)),
                   pl.BlockSpec((tm, nkv), lambda i: (i, 0))),
        compiler_params=_params("parallel"),
        name="mla_prep",
    )(proj, proj, cs, gq, gkv, wq, wkv)


def _flash_sweep(q_s, k_at, v_at, qi, s_ref, p_ref, m_ref, al_ref, acc_ref, *, tq, rc):
    rows = q_s.shape[0]
    tk = tq // 2
    nl = tk // LANES
    m_ref[...] = jnp.full(m_ref.shape, NEG_INF, F32)
    acc_ref[...] = jnp.zeros(acc_ref.shape, F32)
    p_ref[1] = jnp.zeros(p_ref.shape[1:], BF16)
    al_ref[1] = jnp.ones(al_ref.shape[1:], F32)

    full = [(0, rows)]
    late = [(m * tq + tk, (m + 1) * tq) for m in range(rows // tq)]

    def qk(kb, slot, spans):
        k = k_at(kb)
        for a, b in spans:
            s_ref[slot, a:b, :] = lax.dot_general(
                q_s[a:b, :], k, (((1,), (1,)), ((), ())), preferred_element_type=F32)

    def pv(kb, slot, spans):
        v = v_at(kb)
        v1 = jnp.concatenate([v, jnp.ones_like(v)], axis=1)
        for a, b in spans:
            upd = jnp.dot(p_ref[slot, a:b, :], v1, preferred_element_type=F32)
            al = al_ref[slot, a:b, :]
            for j in range(2):
                ls = slice(j * LANES, (j + 1) * LANES)
                acc_ref[a:b, ls] = al * acc_ref[a:b, ls] + upd[:, ls]

    def softmax(s_slot, slot, spans, col0):
        def hidden(r0, j):
            return col0 is not None and col0 + j * LANES > r0 % tq + rc - 1

        def load(r0, j):
            x = s_ref[s_slot, r0:r0 + rc, j * LANES:(j + 1) * LANES]
            if col0 is not None and col0 + (j + 1) * LANES - 1 > r0 % tq:
                row = lax.broadcasted_iota(jnp.int32, x.shape, 0) + (r0 % tq)
                col = lax.broadcasted_iota(jnp.int32, x.shape, 1) + (col0 + j * LANES)
                x = jnp.where(col <= row, x, NEG_INF)
            return x

        chunks = [r0 for a, b in spans for r0 in range(a, b, rc)]
        for r0 in chunks:
            live = [j for j in range(nl) if not hidden(r0, j)]
            pm = load(r0, live[0])
            for j in live[1:]:
                pm = jnp.maximum(pm, load(r0, j))
            mb = jnp.broadcast_to(jnp.max(pm, axis=-1, keepdims=True), (rc, LANES))
            m_old = m_ref[r0:r0 + rc, :]
            m_new = jnp.maximum(m_old, mb)
            al_ref[slot, r0:r0 + rc, :] = jnp.exp2(m_old - m_new)
            m_ref[r0:r0 + rc, :] = m_new

        for r0 in chunks:
            m_new = m_ref[r0:r0 + rc, :]
            for j in range(nl):
                ls = slice(j * LANES, (j + 1) * LANES)
                if hidden(r0, j):
                    p_ref[slot, r0:r0 + rc, ls] = jnp.zeros((rc, LANES), BF16)
                else:
                    p_ref[slot, r0:r0 + rc, ls] = jnp.exp2(load(r0, j) - m_new).astype(BF16)

    qk(0, 0, full)

    def pair(tp, carry):
        e = 2 * tp
        qk(e + 1, 1, full)
        softmax(0, 0, full, None)
        pv(jnp.maximum(e - 1, 0), 1, full)
        qk(e + 2, 0, full)
        softmax(1, 1, full, None)
        pv(e, 0, full)
        return carry

    lax.fori_loop(0, qi, pair, 0)
    e = 2 * qi
    qk(e + 1, 1, late)
    softmax(0, 2, full, 0)
    pv(jnp.maximum(e - 1, 0), 1, full)
    softmax(1, 3, late, tk)
    pv(e, 2, full)
    pv(e + 1, 3, late)


def _flash_scratch(rows, tq, dv):
    tk = tq // 2
    return [pltpu.VMEM((2, rows, tk), F32),
            pltpu.VMEM((4, rows, tk), BF16),
            pltpu.VMEM((rows, LANES), F32),
            pltpu.VMEM((4, rows, LANES), F32),
            pltpu.VMEM((rows, 2 * dv), F32)]


def _diff_attn_kernel(q_ref, k_ref, v_ref, lam_ref, g_ref, o_ref, q_s, *scratch,
                      tq, rc, lambda_init):
    qi = pl.program_id(2)
    q = q_ref[...]
    lane = lax.broadcasted_iota(jnp.int32, q.shape, 1)
    zero = jnp.zeros_like(q)
    q_s[:tq, :] = jnp.where(lane < A_QK_DIM, q, zero)
    q_s[tq:, :] = jnp.where(lane < A_QK_DIM, zero, q)

    tk = tq // 2

    def k_at(kb):
        return k_ref[pl.ds(pl.multiple_of(kb * tk, tk), tk), :]

    def v_at(kb):
        return v_ref[pl.ds(pl.multiple_of(kb * tk, tk), tk), :]

    _flash_sweep(q_s, k_at, v_at, qi, *scratch, tq=tq, rc=rc)
    acc = scratch[-1][...]
    o = acc[:, :A_V_DIM] / acc[:, A_V_DIM:]
    lv = lam_ref[...]
    lam = (jnp.exp(jnp.sum(lv[0:1] * lv[1:2], axis=-1, keepdims=True))
           - jnp.exp(jnp.sum(lv[2:3] * lv[3:4], axis=-1, keepdims=True))
           + lambda_init)
    d = o[:tq] - lam * o[tq:]
    o_ref[...] = (_rms(d, g_ref[...], A_SUBLN_EPS) * (1.0 - lambda_init)).astype(o_ref.dtype)


def _diff_attn(proj, lamv, subln, batch, seq, lambda_init, tq=512, rc=64):
    t = proj.shape[0]
    tq = _tile(seq, tq)
    nq = seq // tq
    qc, kc, vc = C_QA // LANES, C_KA // LANES, C_VA // LANES
    return pl.pallas_call(
        functools.partial(_diff_attn_kernel, tq=tq, rc=rc, lambda_init=lambda_init),
        out_shape=jax.ShapeDtypeStruct((t, A_HEADS * A_V_DIM), BF16),
        grid=(batch, A_HEADS, nq),
        in_specs=[pl.BlockSpec((tq, LANES), lambda b, h, i: (b * nq + i, qc + h)),
                  pl.BlockSpec((seq, LANES), lambda b, h, i: (b, kc + h)),
                  pl.BlockSpec((seq, LANES), lambda b, h, i: (b, vc + h)),
                  pl.BlockSpec(lamv.shape, lambda b, h, i: (0, 0)),
                  pl.BlockSpec((1, A_V_DIM), lambda b, h, i: (0, 0))],
        out_specs=pl.BlockSpec((tq, LANES), lambda b, h, i: (b * nq + i, h)),
        scratch_shapes=[pltpu.VMEM((2 * tq, LANES), BF16)] + _flash_scratch(2 * tq, tq, A_V_DIM),
        compiler_params=_params("parallel", "parallel", "arbitrary"),
        name="diff_attn",
    )(proj, proj, proj, lamv, subln)


def _mla_attn_kernel(q_ref, kn_ref, kr_ref, v_ref, o_ref, *scratch, tq, rc):
    qi = pl.program_id(2)

    tk = tq // 2

    def k_at(kb):
        sl = pl.ds(pl.multiple_of(kb * tk, tk), tk)
        return jnp.concatenate([kn_ref[sl, :], kr_ref[sl, :]], axis=1)

    def v_at(kb):
        return v_ref[pl.ds(pl.multiple_of(kb * tk, tk), tk), :]

    _flash_sweep(q_ref, k_at, v_at, qi, *scratch, tq=tq, rc=rc)
    acc = scratch[-1][...]
    o_ref[...] = (acc[:, :B_V_DIM] / acc[:, B_V_DIM:]).astype(o_ref.dtype)


def _mla_attn(qb, kvb, batch, seq, tq=1024, rc=64):
    t = qb.shape[0]
    tq = _tile(seq, tq)
    nq = seq // tq
    return pl.pallas_call(
        functools.partial(_mla_attn_kernel, tq=tq, rc=rc),
        out_shape=jax.ShapeDtypeStruct((t, B_HEADS * B_V_DIM), BF16),
        grid=(batch, B_HEADS, nq),
        in_specs=[pl.BlockSpec((tq, 2 * LANES), lambda b, h, i: (b * nq + i, h)),
                  pl.BlockSpec((seq, LANES), lambda b, h, i: (b, h)),
                  pl.BlockSpec((seq, LANES), lambda b, h, i: (b, B_HEADS)),
                  pl.BlockSpec((seq, LANES), lambda b, h, i: (b, B_HEADS + 1 + h))],
        out_specs=pl.BlockSpec((tq, LANES), lambda b, h, i: (b * nq + i, h)),
        scratch_shapes=_flash_scratch(tq, tq, B_V_DIM),
        compiler_params=_params("parallel", "parallel", "arbitrary"),
        name="mla_attn",
    )(qb, kvb, kvb, kvb)


def _rglru_kernel(y_ref, x_ref, cw_ref, cb_ref, wg_ref, bg_ref, lam_ref, o_ref,
                  halo_ref, h_ref, *, ts, bw):
    @pl.when(pl.program_id(2) == 0)
    def _():
        halo_ref[...] = jnp.zeros_like(halo_ref)
        h_ref[...] = jnp.zeros_like(h_ref)

    x = x_ref[...].astype(F32)
    ext = jnp.concatenate([halo_ref[...], x], axis=0)
    halo_ref[...] = x[ts - SUBLANES:]
    cw = cw_ref[...]
    xc = cb_ref[...] + cw[CONV_WIDTH - 1:CONV_WIDTH] * x
    for k in range(1, CONV_WIDTH):
        w = cw[CONV_WIDTH - 1 - k:CONV_WIDTH - k]
        xc = xc + w * pltpu.roll(ext, k, 0)[SUBLANES:]

    gates = jnp.dot(xc.astype(BF16), wg_ref[...], preferred_element_type=F32) + bg_ref[...]
    r = jax.nn.sigmoid(gates[:, :bw])
    i = jax.nn.sigmoid(gates[:, bw:])
    log_a = -LRU_C * r * jax.nn.softplus(-lam_ref[...])
    a = jnp.exp(log_a)
    b = jnp.sqrt(-jnp.tanh(log_a) * (a * a + 1.0)) * (i * xc)

    ng = ts // SUBLANES
    a3 = a.reshape(ng, SUBLANES, bw)
    b3 = b.reshape(ng, SUBLANES, bw)
    sub = lax.broadcasted_iota(jnp.int32, a3.shape, 1)
    for k in (1, 2, 4):
        keep = sub >= k
        b3 = jnp.where(keep, a3 * pltpu.roll(b3, k, 1) + b3, b3)
        a3 = jnp.where(keep, a3 * pltpu.roll(a3, k, 1), a3)

    h = h_ref[...]
    rows = []
    for g in range(ng):
        hv = a3[g] * h + b3[g]
        rows.append(hv)
        h = hv[SUBLANES - 1:SUBLANES]
    h_ref[...] = h
    hs = jnp.concatenate(rows, axis=0)
    y = jax.nn.gelu(y_ref[...].astype(F32), approximate=True)
    o_ref[...] = (y * hs).astype(o_ref.dtype)


def _rglru(proj, conv_w, conv_b, wg, bg, lam, batch, seq, ts=256):
    t = proj.shape[0]
    width = proj.shape[1] // 2
    bw = width // LRU_BLOCKS
    ts = _tile(seq, ts)
    ns = seq // ts
    return pl.pallas_call(
        functools.partial(_rglru_kernel, ts=ts, bw=bw),
        out_shape=jax.ShapeDtypeStruct((t, width), BF16),
        grid=(batch, LRU_BLOCKS, ns),
        in_specs=[pl.BlockSpec((ts, bw), lambda b, n, s: (b * ns + s, n)),
                  pl.BlockSpec((ts, bw), lambda b, n, s: (b * ns + s, LRU_BLOCKS + n)),
                  pl.BlockSpec((CONV_WIDTH, bw), lambda b, n, s: (0, n)),
                  pl.BlockSpec((1, bw), lambda b, n, s: (0, n)),
                  pl.BlockSpec((None, bw, 2 * bw), lambda b, n, s: (n, 0, 0)),
                  pl.BlockSpec((None, 1, 2 * bw), lambda b, n, s: (n, 0, 0)),
                  pl.BlockSpec((1, bw), lambda b, n, s: (0, n))],
        out_specs=pl.BlockSpec((ts, bw), lambda b, n, s: (b * ns + s, n)),
        scratch_shapes=[pltpu.VMEM((SUBLANES, bw), F32), pltpu.VMEM((1, bw), F32)],
        compiler_params=_params("parallel", "parallel", "arbitrary"),
        name="rglru",
    )(proj, proj, conv_w, conv_b, wg, bg, lam)


def _rot_half_cols(w):
    half = w.shape[-1] // 2
    return jnp.concatenate([-w[..., half:], w[..., :half]], axis=-1)


def _prep_attn_w_in(w):
    o0 = A_HEADS * 2 * A_QK_DIM
    o1 = 2 * o0
    o2 = o1 + A_HEADS * A_V_DIM
    o3 = o2 + B_Q_LORA
    o4 = o3 + B_KV_LORA
    kr = w[:, o4:]
    return jnp.concatenate(
        [w[:, o2:o3], w[:, o3:o4], kr, _rot_half_cols(kr),
         w[:, :o0] * (A_QK_DIM ** -0.5 * LOG2E), w[:, o0:o1], w[:, o1:o2]],
        axis=1).astype(BF16)


def _prep_w_uq(w):
    k = w.shape[0]
    w = w.reshape(k, B_HEADS, B_QK_DIM)
    rope = w[:, :, B_NOPE_DIM:]
    return jnp.concatenate([w[:, :, :B_NOPE_DIM], rope, _rot_half_cols(rope)],
                           axis=-1).reshape(k, B_HEADS * 2 * LANES).astype(BF16)


def _prep_w_ukv(w):
    k = w.shape[0]
    w = w.reshape(k, B_HEADS, B_NOPE_DIM + B_V_DIM)
    return jnp.concatenate([w[:, :, :B_NOPE_DIM].reshape(k, -1),
                            w[:, :, B_NOPE_DIM:].reshape(k, -1)], axis=1).astype(BF16)


def kernel(x, positions, norm_mix, norm_mlp, norm_final, attn_w_in, attn_lambda_q1,
           attn_lambda_k1, attn_lambda_q2, attn_lambda_k2, attn_subln, attn_q_norm,
           attn_kv_norm, attn_w_uq, attn_w_ukv, attn_w_out, rec_w_in, rec_conv_w,
           rec_conv_b, rec_w_a, rec_b_a, rec_w_x, rec_b_x, rec_lambda, rec_w_out,
           mlp_w1, mlp_w2):
    batch, seq, d = x.shape
    t = batch * seq
    h0 = x.reshape(t, d)
    row = lambda v: v.reshape(1, -1)

    half = B_ROPE_DIM // 2
    inv_freq = 1.0 / (ROPE_THETA ** (jnp.arange(0, B_ROPE_DIM, 2, dtype=F32) / B_ROPE_DIM))
    invf = jnp.tile(inv_freq, LANES // half).reshape(1, LANES)
    pos_lanes = jnp.broadcast_to(positions.astype(F32).reshape(t, 1), (t, LANES))
    cs = _rope_table(pos_lanes, invf)

    proj = _norm_matmul(h0, row(norm_mix[0]), _prep_attn_w_in(attn_w_in[0]), BF16,
                        "attn_in_proj")
    qb, kvb = _mla_prep(proj, cs, row(attn_q_norm[0]), row(attn_kv_norm[0]),
                        _prep_w_uq(attn_w_uq[0]), _prep_w_ukv(attn_w_ukv[0]))
    lambda_init = 0.8 - 0.6 * math.exp(-0.3 * 0)
    lamv = jnp.stack([attn_lambda_q1[0], attn_lambda_k1[0],
                      attn_lambda_q2[0], attn_lambda_k2[0]]).astype(F32)
    oa = _diff_attn(proj, lamv, row(attn_subln[0]), batch, seq, lambda_init)
    ob = _mla_attn(qb, kvb, batch, seq)
    w_out = attn_w_out[0].astype(BF16)
    na = A_HEADS * A_V_DIM
    h1 = _matmul_res([oa, ob], [w_out[:na], w_out[na:]], h0, "attn_out_proj")
    h2 = _mlp(h1, row(norm_mlp[0]), mlp_w1[0].astype(BF16), mlp_w2[0].astype(BF16),
              row(norm_final), False, "mlp0")

    proj2 = _norm_matmul(h2, row(norm_mix[1]), rec_w_in[0].astype(BF16), BF16,
                         "rec_in_proj", tn=512)
    wg = jnp.concatenate([rec_w_a[0], rec_w_x[0]], axis=-1).astype(BF16)
    bg = jnp.concatenate([rec_b_a[0], rec_b_x[0]], axis=-1)[:, None, :]
    g = _rglru(proj2, rec_conv_w[0], row(rec_conv_b[0]), wg, bg, row(rec_lambda[0]),
               batch, seq)
    h3 = _matmul_res([g], [rec_w_out[0].astype(BF16)], h2, "rec_out_proj")
    out = _mlp(h3, row(norm_mlp[1]), mlp_w1[1].astype(BF16), mlp_w2[1].astype(BF16),
               row(norm_final), True, "mlp1")
    return out.reshape(batch, seq, d)
```

```python
import functools
import math

import jax
import jax.numpy as jnp
from jax import lax
from jax.experimental import pallas as pl
from jax.experimental.pallas import tpu as pltpu

F32 = jnp.float32
BF16 = jnp.bfloat16

NORM_EPS = 1e-6
NEG_INF = -1e30
A_HEADS = 8
A_QK_DIM = 64
A_V_DIM = 128
A_SUBLN_EPS = 1e-5
B_HEADS = 8
B_Q_LORA = 768
B_KV_LORA = 512
B_NOPE_DIM = 128
B_ROPE_DIM = 64
B_V_DIM = 128
B_QK_DIM = B_NOPE_DIM + B_ROPE_DIM
ROPE_THETA = 10000.0
LRU_BLOCKS = 8
CONV_WIDTH = 4
LRU_C = 8.0
LOG2E = math.log2(math.e)

LANES = 128
SUBLANES = 8
VMEM_LIMIT = 56 * 1024 * 1024

C_CQ = 0
C_CKV = C_CQ + B_Q_LORA
C_KR = C_CKV + B_KV_LORA
C_QA = C_KR + LANES
C_KA = C_QA + A_HEADS * 2 * A_QK_DIM
C_VA = C_KA + A_HEADS * 2 * A_QK_DIM
C_END = C_VA + A_HEADS * A_V_DIM
MXU_N = 256
C_PAD = -(-C_END // (3 * MXU_N)) * (3 * MXU_N)


def _params(*sem):
    return pltpu.CompilerParams(dimension_semantics=sem, vmem_limit_bytes=VMEM_LIMIT)


def _tile(n, pref):
    t = min(n, pref)
    assert n % t == 0, (n, t)
    return t


def _rms(x, g, eps):
    return x * lax.rsqrt(jnp.mean(x * x, axis=-1, keepdims=True) + eps) * g


def _rope_table_kernel(pos_ref, invf_ref, o_ref):
    ang = pos_ref[...] * invf_ref[...]
    lane = lax.broadcasted_iota(jnp.int32, ang.shape, 1)
    o_ref[...] = jnp.where(lane < B_ROPE_DIM, jnp.cos(ang), jnp.sin(ang))


def _rope_table(pos_lanes, invf):
    t = pos_lanes.shape[0]
    tm = _tile(t, 1024)
    return pl.pallas_call(
        _rope_table_kernel,
        out_shape=jax.ShapeDtypeStruct((t, LANES), F32),
        grid=(t // tm,),
        in_specs=[pl.BlockSpec((tm, LANES), lambda i: (i, 0)),
                  pl.BlockSpec((1, LANES), lambda i: (0, 0))],
        out_specs=pl.BlockSpec((tm, LANES), lambda i: (i, 0)),
        compiler_params=_params("parallel"),
        name="rope_table",
    )(pos_lanes, invf)


def _norm_matmul_kernel(x_ref, g_ref, w_ref, o_ref, xn_ref):
    @pl.when(pl.program_id(1) == 0)
    def _():
        xn_ref[...] = _rms(x_ref[...], g_ref[...], NORM_EPS).astype(BF16)

    o_ref[...] = jnp.dot(xn_ref[...], w_ref[...],
                         preferred_element_type=F32).astype(o_ref.dtype)


def _norm_matmul(x, g, w, layer, out_dtype, name, tm=1024, tn=768):
    t, k = x.shape
    n = w.shape[2]
    tm = _tile(t, tm)
    tn = _tile(n, tn)
    return pl.pallas_call(
        _norm_matmul_kernel,
        out_shape=jax.ShapeDtypeStruct((t, n), out_dtype),
        grid=(t // tm, n // tn),
        in_specs=[pl.BlockSpec((tm, k), lambda i, j: (i, 0)),
                  pl.BlockSpec((1, k), lambda i, j: (0, 0)),
                  pl.BlockSpec((None, k, tn), lambda i, j: (layer, 0, j))],
        out_specs=pl.BlockSpec((tm, tn), lambda i, j: (i, j)),
        scratch_shapes=[pltpu.VMEM((tm, k), BF16)],
        compiler_params=_params("parallel", "arbitrary"),
        name=name,
    )(x, g, w)


def _matmul_res_kernel(*refs, n_lhs):
    lhs = refs[:n_lhs]
    ws = refs[n_lhs:2 * n_lhs]
    res_ref, o_ref = refs[2 * n_lhs], refs[2 * n_lhs + 1]
    acc = res_ref[...]
    for a, w in zip(lhs, ws):
        acc = acc + jnp.dot(a[...], w[...], preferred_element_type=F32)
    o_ref[...] = acc


def _matmul_res(lhs_list, w, layer, res, name, tm=1024, tn=1024):
    t, n = res.shape
    tm = _tile(t, tm)
    tn = _tile(n, tn)
    n_lhs = len(lhs_list)
    kp = lhs_list[0].shape[1]
    assert all(a.shape[1] == kp for a in lhs_list) and n_lhs * kp == w.shape[1]
    in_specs = []
    for a in lhs_list:
        in_specs.append(pl.BlockSpec((tm, kp), lambda i, j: (i, 0)))
    for part in range(n_lhs):
        in_specs.append(pl.BlockSpec(
            (None, kp, tn), functools.partial(lambda i, j, part: (layer, part, j), part=part)))
    in_specs.append(pl.BlockSpec((tm, tn), lambda i, j: (i, j)))
    return pl.pallas_call(
        functools.partial(_matmul_res_kernel, n_lhs=n_lhs),
        out_shape=jax.ShapeDtypeStruct((t, n), F32),
        grid=(t // tm, n // tn),
        in_specs=in_specs,
        out_specs=pl.BlockSpec((tm, tn), lambda i, j: (i, j)),
        compiler_params=_params("parallel", "parallel"),
        name=name,
    )(*lhs_list, *([w] * n_lhs), res)


def _mlp_kernel(h_ref, g_ref, w1_ref, w2_ref, gf_ref, o_ref, xn_ref, *, final_norm):
    j = pl.program_id(1)

    @pl.when(j == 0)
    def _():
        h = h_ref[...]
        xn_ref[...] = _rms(h, g_ref[...], NORM_EPS).astype(BF16)
        o_ref[...] = h

    u = jnp.maximum(jnp.dot(xn_ref[...], w1_ref[...], preferred_element_type=F32), 0.0)
    u = (u * u).astype(BF16)
    o_ref[...] += jnp.dot(u, w2_ref[...], preferred_element_type=F32)

    if final_norm:
        @pl.when(j == pl.num_programs(1) - 1)
        def _():
            o_ref[...] = _rms(o_ref[...], gf_ref[...], NORM_EPS)


def _mlp(h, g, w1, w2, layer, gf, final_norm, name, tm=1024, tf=512):
    t, d = h.shape
    f = w1.shape[2]
    tm = _tile(t, tm)
    tf = _tile(f, tf)
    return pl.pallas_call(
        functools.partial(_mlp_kernel, final_norm=final_norm),
        out_shape=jax.ShapeDtypeStruct((t, d), F32),
        grid=(t // tm, f // tf),
        in_specs=[pl.BlockSpec((tm, d), lambda i, j: (i, 0)),
                  pl.BlockSpec((1, d), lambda i, j: (0, 0)),
                  pl.BlockSpec((None, d, tf), lambda i, j: (layer, 0, j)),
                  pl.BlockSpec((None, tf, d), lambda i, j: (layer, j, 0)),
                  pl.BlockSpec((1, d), lambda i, j: (0, 0))],
        out_specs=pl.BlockSpec((tm, d), lambda i, j: (i, 0)),
        scratch_shapes=[pltpu.VMEM((tm, d), BF16)],
        compiler_params=_params("parallel", "arbitrary"),
        name=name,
    )(h, g, w1, w2, gf)


def _mla_prep_kernel(lat_ref, kr_ref, cs_ref, gq_ref, gkv_ref, wq_ref, wkv_ref,
                     q_ref, kv_ref):
    lat = lat_ref[...].astype(F32)
    cs = cs_ref[...]
    scale = B_QK_DIM ** -0.5 * LOG2E
    cqn = _rms(lat[:, :B_Q_LORA], gq_ref[...], NORM_EPS).astype(BF16)
    q = jnp.dot(cqn, wq_ref[...], preferred_element_type=F32)
    cs_scaled = cs * scale
    for h in range(B_HEADS):
        lo = h * 2 * LANES
        q_ref[:, lo:lo + LANES] = (q[:, lo:lo + LANES] * scale).astype(BF16)
        q_ref[:, lo + LANES:lo + 2 * LANES] = (
            q[:, lo + LANES:lo + 2 * LANES] * cs_scaled).astype(BF16)
    ckvn = _rms(lat[:, B_Q_LORA:], gkv_ref[...], NORM_EPS).astype(BF16)
    kv = jnp.dot(ckvn, wkv_ref[...], preferred_element_type=F32)
    nk = B_HEADS * B_NOPE_DIM
    kv_ref[:, :nk] = kv[:, :nk].astype(BF16)
    kv_ref[:, nk + LANES:] = kv[:, nk:].astype(BF16)
    t = kr_ref[...].astype(F32) * cs
    kv_ref[:, nk:nk + LANES] = (t + pltpu.roll(t, B_ROPE_DIM, 1)).astype(BF16)


def _mla_prep(proj, cs, gq, gkv, wq, wkv, tm=512):
    t = proj.shape[0]
    tm = _tile(t, tm)
    lat_w = B_Q_LORA + B_KV_LORA
    nq = B_HEADS * 2 * LANES
    nkv = B_HEADS * (B_NOPE_DIM + B_V_DIM) + LANES
    return pl.pallas_call(
        _mla_prep_kernel,
        out_shape=(jax.ShapeDtypeStruct((t, nq), BF16),
                   jax.ShapeDtypeStruct((t, nkv), BF16)),
        grid=(t // tm,),
        in_specs=[pl.BlockSpec((tm, lat_w), lambda i: (i, 0)),
                  pl.BlockSpec((tm, LANES), lambda i: (i, C_KR // LANES)),
                  pl.BlockSpec((tm, LANES), lambda i: (i, 0)),
                  pl.BlockSpec((1, B_Q_LORA), lambda i: (0, 0)),
                  pl.BlockSpec((1, B_KV_LORA), lambda i: (0, 0)),
                  pl.BlockSpec(wq.shape, lambda i: (0, 0)),
                  pl.BlockSpec(wkv.shape, lambda i: (0, 0))],
        out_specs=(pl.BlockSpec((tm, nq), lambda i: (i, 0)),
                   pl.BlockSpec((tm, nkv), lambda i: (i, 0))),
        compiler_params=_params("parallel"),
        name="mla_prep",
    )(proj, proj, cs, gq, gkv, wq, wkv)


def _flash_sweep(q_s, k_at, v_at, qi, s_ref, p_ref, m_ref, al_ref, acc_ref, *, tq, rc):
    rows = q_s.shape[0]
    tk = tq // 2
    nl = tk // LANES

    full = [(0, rows)]
    late = [(m * tq + tk, (m + 1) * tq) for m in range(rows // tq)]

    def qk(kb, slot, spans):
        k = k_at(kb)
        for a, b in spans:
            s_ref[slot, a:b, :] = lax.dot_general(
                q_s[a:b, :], k, (((1,), (1,)), ((), ())), preferred_element_type=F32)

    def pv(kb, slot, spans):
        v = v_at(kb)
        v1 = jnp.concatenate([v, jnp.ones_like(v)], axis=1)
        for a, b in spans:
            upd = jnp.dot(p_ref[slot, a:b, :], v1, preferred_element_type=F32)
            al = al_ref[slot, a:b, :]
            for j in range(2):
                ls = slice(j * LANES, (j + 1) * LANES)
                acc_ref[a:b, ls] = al * acc_ref[a:b, ls] + upd[:, ls]

    def softmax(s_slot, slot, spans, col0):
        def hidden(r0, j):
            return col0 is not None and col0 + j * LANES > r0 % tq + rc - 1

        def load(r0, j):
            x = s_ref[s_slot, r0:r0 + rc, j * LANES:(j + 1) * LANES]
            if col0 is not None and col0 + (j + 1) * LANES - 1 > r0 % tq:
                row = lax.broadcasted_iota(jnp.int32, x.shape, 0) + (r0 % tq)
                col = lax.broadcasted_iota(jnp.int32, x.shape, 1) + (col0 + j * LANES)
                x = jnp.where(col <= row, x, NEG_INF)
            return x

        chunks = [r0 for a, b in spans for r0 in range(a, b, rc)]
        for r0 in chunks:
            live = [j for j in range(nl) if not hidden(r0, j)]
            pm = load(r0, live[0])
            for j in live[1:]:
                pm = jnp.maximum(pm, load(r0, j))
            mb = jnp.broadcast_to(jnp.max(pm, axis=-1, keepdims=True), (rc, LANES))
            m_old = m_ref[r0:r0 + rc, :]
            m_new = jnp.maximum(m_old, mb)
            al_ref[slot, r0:r0 + rc, :] = jnp.exp2(m_old - m_new)
            m_ref[r0:r0 + rc, :] = m_new

        for r0 in chunks:
            m_new = m_ref[r0:r0 + rc, :]
            for j in range(nl):
                ls = slice(j * LANES, (j + 1) * LANES)
                if hidden(r0, j):
                    p_ref[slot, r0:r0 + rc, ls] = jnp.zeros((rc, LANES), BF16)
                else:
                    p_ref[slot, r0:r0 + rc, ls] = jnp.exp2(load(r0, j) - m_new).astype(BF16)

    qk(0, 0, full)
    m_ref[...] = jnp.full(m_ref.shape, NEG_INF, F32)
    acc_ref[...] = jnp.zeros(acc_ref.shape, F32)
    p_ref[1] = jnp.zeros(p_ref.shape[1:], BF16)
    al_ref[1] = jnp.ones(al_ref.shape[1:], F32)

    def pair(tp, carry):
        e = 2 * tp
        qk(e + 1, 1, full)
        softmax(0, 0, full, None)
        pv(jnp.maximum(e - 1, 0), 1, full)
        qk(e + 2, 0, full)
        softmax(1, 1, full, None)
        pv(e, 0, full)
        return carry

    lax.fori_loop(0, qi, pair, 0)
    e = 2 * qi
    qk(e + 1, 1, late)
    softmax(0, 2, full, 0)
    pv(jnp.maximum(e - 1, 0), 1, full)
    softmax(1, 3, late, tk)
    pv(e, 2, full)
    pv(e + 1, 3, late)


def _flash_scratch(rows, tq, dv):
    tk = tq // 2
    return [pltpu.VMEM((2, rows, tk), F32),
            pltpu.VMEM((4, rows, tk), BF16),
            pltpu.VMEM((rows, LANES), F32),
            pltpu.VMEM((4, rows, LANES), F32),
            pltpu.VMEM((rows, 2 * dv), F32)]


def _diff_attn_kernel(q_ref, k_ref, v_ref, lam_ref, g_ref, o_ref, q_s, *scratch,
                      tq, rc, lambda_init):
    qi = pl.program_id(2)
    q = q_ref[...]
    lane = lax.broadcasted_iota(jnp.int32, q.shape, 1)
    zero = jnp.zeros_like(q)
    q_s[:tq, :] = jnp.where(lane < A_QK_DIM, q, zero)
    q_s[tq:, :] = jnp.where(lane < A_QK_DIM, zero, q)

    tk = tq // 2

    def k_at(kb):
        return k_ref[pl.ds(pl.multiple_of(kb * tk, tk), tk), :]

    def v_at(kb):
        return v_ref[pl.ds(pl.multiple_of(kb * tk, tk), tk), :]

    _flash_sweep(q_s, k_at, v_at, qi, *scratch, tq=tq, rc=rc)
    acc = scratch[-1][...]
    o = acc[:, :A_V_DIM] / acc[:, A_V_DIM:]
    lv = lam_ref[...]
    lam = (jnp.exp(jnp.sum(lv[0:1] * lv[1:2], axis=-1, keepdims=True))
           - jnp.exp(jnp.sum(lv[2:3] * lv[3:4], axis=-1, keepdims=True))
           + lambda_init)
    d = o[:tq] - lam * o[tq:]
    o_ref[...] = (_rms(d, g_ref[...], A_SUBLN_EPS) * (1.0 - lambda_init)).astype(o_ref.dtype)


def _diff_attn(proj, lamv, subln, batch, seq, lambda_init, tq=512, rc=64):
    t = proj.shape[0]
    tq = _tile(seq, tq)
    nq = seq // tq
    qc, kc, vc = C_QA // LANES, C_KA // LANES, C_VA // LANES
    return pl.pallas_call(
        functools.partial(_diff_attn_kernel, tq=tq, rc=rc, lambda_init=lambda_init),
        out_shape=jax.ShapeDtypeStruct((t, A_HEADS * A_V_DIM), BF16),
        grid=(batch, A_HEADS, nq),
        in_specs=[pl.BlockSpec((tq, LANES), lambda b, h, i: (b * nq + i, qc + h)),
                  pl.BlockSpec((seq, LANES), lambda b, h, i: (b, kc + h)),
                  pl.BlockSpec((seq, LANES), lambda b, h, i: (b, vc + h)),
                  pl.BlockSpec(lamv.shape, lambda b, h, i: (0, 0)),
                  pl.BlockSpec((1, A_V_DIM), lambda b, h, i: (0, 0))],
        out_specs=pl.BlockSpec((tq, LANES), lambda b, h, i: (b * nq + i, h)),
        scratch_shapes=[pltpu.VMEM((2 * tq, LANES), BF16)] + _flash_scratch(2 * tq, tq, A_V_DIM),
        compiler_params=_params("parallel", "parallel", "arbitrary"),
        name="diff_attn",
    )(proj, proj, proj, lamv, subln)


def _mla_attn_kernel(q_ref, kn_ref, kr_ref, v_ref, o_ref, *scratch, tq, rc):
    qi = pl.program_id(2)

    tk = tq // 2

    def k_at(kb):
        sl = pl.ds(pl.multiple_of(kb * tk, tk), tk)
        return jnp.concatenate([kn_ref[sl, :], kr_ref[sl, :]], axis=1)

    def v_at(kb):
        return v_ref[pl.ds(pl.multiple_of(kb * tk, tk), tk), :]

    _flash_sweep(q_ref, k_at, v_at, qi, *scratch, tq=tq, rc=rc)
    acc = scratch[-1][...]
    o_ref[...] = (acc[:, :B_V_DIM] / acc[:, B_V_DIM:]).astype(o_ref.dtype)


def _mla_attn(qb, kvb, batch, seq, tq=1024, rc=64):
    t = qb.shape[0]
    tq = _tile(seq, tq)
    nq = seq // tq
    return pl.pallas_call(
        functools.partial(_mla_attn_kernel, tq=tq, rc=rc),
        out_shape=jax.ShapeDtypeStruct((t, B_HEADS * B_V_DIM), BF16),
        grid=(batch, B_HEADS, nq),
        in_specs=[pl.BlockSpec((tq, 2 * LANES), lambda b, h, i: (b * nq + i, h)),
                  pl.BlockSpec((seq, LANES), lambda b, h, i: (b, h)),
                  pl.BlockSpec((seq, LANES), lambda b, h, i: (b, B_HEADS)),
                  pl.BlockSpec((seq, LANES), lambda b, h, i: (b, B_HEADS + 1 + h))],
        out_specs=pl.BlockSpec((tq, LANES), lambda b, h, i: (b * nq + i, h)),
        scratch_shapes=_flash_scratch(tq, tq, B_V_DIM),
        compiler_params=_params("parallel", "parallel", "arbitrary"),
        name="mla_attn",
    )(qb, kvb, kvb, kvb)


def _rglru_kernel(y_ref, x_ref, cw_ref, cb_ref, wg_ref, bg_ref, lam_ref, o_ref,
                  halo_ref, h_ref, xs_ref, hs_ref, *, ts, bw):
    seg = ts // SUBLANES
    pitch = xs_ref.shape[1] // SUBLANES
    nlc = bw // LANES

    def put(ref, r0, v):
        for c in range(nlc):
            ref[c, r0:r0 + v.shape[0], :] = v[:, c * LANES:(c + 1) * LANES]

    def take(ref, start, stride):
        return jnp.concatenate(
            [ref[c, pl.ds(start, SUBLANES, stride=stride), :] for c in range(nlc)], axis=1)

    @pl.when(pl.program_id(2) == 0)
    def _():
        halo_ref[...] = jnp.zeros_like(halo_ref)
        h_ref[...] = jnp.zeros_like(h_ref)

    x = x_ref[...].astype(F32)
    for s in range(SUBLANES):
        put(xs_ref, s * pitch, x[s * seg:(s + 1) * seg])
    xp = [take(xs_ref, g, pitch) for g in range(seg)]
    sub = lax.broadcasted_iota(jnp.int32, (SUBLANES, bw), 0)
    halo = halo_ref[...]
    before = {}
    for k in range(1, CONV_WIDTH):
        prev_tile = halo[SUBLANES - k:SUBLANES - k + 1]
        before[-k] = jnp.where(sub == 0, prev_tile, pltpu.roll(xp[seg - k], 1, 0))
    halo_ref[...] = x[ts - SUBLANES:]
    cw = cw_ref[...]
    cb = cb_ref[...]
    xc = []
    for g in range(seg):
        acc = cb + cw[CONV_WIDTH - 1:CONV_WIDTH] * xp[g]
        for k in range(1, CONV_WIDTH):
            src = xp[g - k] if g - k >= 0 else before[g - k]
            acc = acc + cw[CONV_WIDTH - 1 - k:CONV_WIDTH - k] * src
        xc.append(acc)
    xc = jnp.concatenate(xc, axis=0)

    gates = jnp.dot(xc.astype(BF16), wg_ref[...], preferred_element_type=F32) + bg_ref[...]
    r = jax.nn.sigmoid(gates[:, :bw])
    i = jax.nn.sigmoid(gates[:, bw:])
    log_a = -LRU_C * r * jax.nn.softplus(-lam_ref[...])
    a = jnp.exp(log_a)
    b = jnp.sqrt(-jnp.tanh(log_a) * (a * a + 1.0)) * (i * xc)

    hl, ap = [], []
    h = jnp.zeros((SUBLANES, bw), F32)
    prod = jnp.ones((SUBLANES, bw), F32)
    for g in range(seg):
        ag = a[g * SUBLANES:(g + 1) * SUBLANES]
        h = ag * h + b[g * SUBLANES:(g + 1) * SUBLANES]
        prod = ag * prod
        hl.append(h)
        ap.append(prod)
    c = h_ref[...]
    carry = []
    for s in range(SUBLANES):
        carry.append(c)
        c = prod[s:s + 1] * c + h[s:s + 1]
    h_ref[...] = c
    carry = jnp.concatenate(carry, axis=0)
    for g in range(seg):
        put(hs_ref, g * SUBLANES, hl[g] + ap[g] * carry)
    rows = []
    for n in range(ts // SUBLANES):
        start = ((n * SUBLANES) % seg) * SUBLANES + (n * SUBLANES) // seg
        rows.append(take(hs_ref, start, SUBLANES))
    hs = jnp.concatenate(rows, axis=0)
    y = jax.nn.gelu(y_ref[...].astype(F32), approximate=True)
    o_ref[...] = (y * hs).astype(o_ref.dtype)


def _rglru(proj, conv_w, conv_b, wg, bg, lam, batch, seq, ts=1024):
    t = proj.shape[0]
    width = proj.shape[1] // 2
    bw = width // LRU_BLOCKS
    ts = _tile(seq, ts)
    ns = seq // ts
    return pl.pallas_call(
        functools.partial(_rglru_kernel, ts=ts, bw=bw),
        out_shape=jax.ShapeDtypeStruct((t, width), BF16),
        grid=(batch, LRU_BLOCKS, ns),
        in_specs=[pl.BlockSpec((ts, bw), lambda b, n, s: (b * ns + s, n)),
                  pl.BlockSpec((ts, bw), lambda b, n, s: (b * ns + s, LRU_BLOCKS + n)),
                  pl.BlockSpec((CONV_WIDTH, bw), lambda b, n, s: (0, n)),
                  pl.BlockSpec((1, bw), lambda b, n, s: (0, n)),
                  pl.BlockSpec((None, bw, 2 * bw), lambda b, n, s: (n, 0, 0)),
                  pl.BlockSpec((None, 1, 2 * bw), lambda b, n, s: (n, 0, 0)),
                  pl.BlockSpec((1, bw), lambda b, n, s: (0, n))],
        out_specs=pl.BlockSpec((ts, bw), lambda b, n, s: (b * ns + s, n)),
        scratch_shapes=[pltpu.VMEM((SUBLANES, bw), F32), pltpu.VMEM((1, bw), F32),
                        pltpu.VMEM((bw // LANES, ts + SUBLANES * SUBLANES, LANES), F32),
                        pltpu.VMEM((bw // LANES, ts, LANES), F32)],
        compiler_params=_params("parallel", "parallel", "arbitrary"),
        name="rglru",
    )(proj, proj, conv_w, conv_b, wg, bg, lam)


def _rot_half_cols(w):
    half = w.shape[-1] // 2
    return jnp.concatenate([-w[..., half:], w[..., :half]], axis=-1)


def _prep_attn_w_in(w):
    o0 = A_HEADS * 2 * A_QK_DIM
    o1 = 2 * o0
    o2 = o1 + A_HEADS * A_V_DIM
    o3 = o2 + B_Q_LORA
    o4 = o3 + B_KV_LORA
    kr = w[:, o4:]
    return jnp.concatenate(
        [w[:, o2:o3], w[:, o3:o4], kr, _rot_half_cols(kr),
         w[:, :o0] * (A_QK_DIM ** -0.5 * LOG2E), w[:, o0:o1], w[:, o1:o2],
         jnp.zeros((w.shape[0], C_PAD - C_END), w.dtype)], axis=1).astype(BF16)


def _prep_w_uq(w):
    k = w.shape[0]
    w = w.reshape(k, B_HEADS, B_QK_DIM)
    rope = w[:, :, B_NOPE_DIM:]
    return jnp.concatenate([w[:, :, :B_NOPE_DIM], rope, _rot_half_cols(rope)],
                           axis=-1).reshape(k, B_HEADS * 2 * LANES).astype(BF16)


def _prep_w_ukv(w):
    k = w.shape[0]
    w = w.reshape(k, B_HEADS, B_NOPE_DIM + B_V_DIM)
    return jnp.concatenate([w[:, :, :B_NOPE_DIM].reshape(k, -1),
                            w[:, :, B_NOPE_DIM:].reshape(k, -1)], axis=1).astype(BF16)


def kernel(x, positions, norm_mix, norm_mlp, norm_final, attn_w_in, attn_lambda_q1,
           attn_lambda_k1, attn_lambda_q2, attn_lambda_k2, attn_subln, attn_q_norm,
           attn_kv_norm, attn_w_uq, attn_w_ukv, attn_w_out, rec_w_in, rec_conv_w,
           rec_conv_b, rec_w_a, rec_b_a, rec_w_x, rec_b_x, rec_lambda, rec_w_out,
           mlp_w1, mlp_w2):
    batch, seq, d = x.shape
    t = batch * seq
    h0 = x.reshape(t, d)
    row = lambda v: v.reshape(1, -1)

    half = B_ROPE_DIM // 2
    inv_freq = 1.0 / (ROPE_THETA ** (jnp.arange(0, B_ROPE_DIM, 2, dtype=F32) / B_ROPE_DIM))
    invf = jnp.tile(inv_freq, LANES // half).reshape(1, LANES)
    pos_lanes = jnp.broadcast_to(positions.astype(F32).reshape(t, 1), (t, LANES))
    cs = _rope_table(pos_lanes, invf)

    proj = _norm_matmul(h0, row(norm_mix[0]), _prep_attn_w_in(attn_w_in[0])[None], 0, BF16,
                        "attn_in_proj")
    qb, kvb = _mla_prep(proj, cs, row(attn_q_norm[0]), row(attn_kv_norm[0]),
                        _prep_w_uq(attn_w_uq[0]), _prep_w_ukv(attn_w_ukv[0]))
    lambda_init = 0.8 - 0.6 * math.exp(-0.3 * 0)
    lamv = jnp.stack([attn_lambda_q1[0], attn_lambda_k1[0],
                      attn_lambda_q2[0], attn_lambda_k2[0]]).astype(F32)
    oa = _diff_attn(proj, lamv, row(attn_subln[0]), batch, seq, lambda_init)
    ob = _mla_attn(qb, kvb, batch, seq)
    h1 = _matmul_res([oa, ob], attn_w_out.astype(BF16), 0, h0, "attn_out_proj")
    w1 = mlp_w1.astype(BF16)
    w2 = mlp_w2.astype(BF16)
    h2 = _mlp(h1, row(norm_mlp[0]), w1, w2, 0, row(norm_final), False, "mlp0")

    proj2 = _norm_matmul(h2, row(norm_mix[1]), rec_w_in.astype(BF16), 0, BF16,
                         "rec_in_proj", tn=1024)
    wg = jnp.concatenate([rec_w_a[0], rec_w_x[0]], axis=-1).astype(BF16)
    bg = jnp.concatenate([rec_b_a[0], rec_b_x[0]], axis=-1)[:, None, :]
    g = _rglru(proj2, rec_conv_w[0], row(rec_conv_b[0]), wg, bg, row(rec_lambda[0]),
               batch, seq)
    h3 = _matmul_res([g], rec_w_out.astype(BF16), 0, h2, "rec_out_proj")
    out = _mlp(h3, row(norm_mlp[1]), w1, w2, 1, row(norm_final), True, "mlp1")
    return out.reshape(batch, seq, d)
```

```python
import functools
import math

import jax
import jax.numpy as jnp
from jax import lax
from jax.experimental import pallas as pl
from jax.experimental.pallas import tpu as pltpu

F32 = jnp.float32
BF16 = jnp.bfloat16

NORM_EPS = 1e-6
NEG_INF = -1e30
A_HEADS = 8
A_QK_DIM = 64
A_V_DIM = 128
A_SUBLN_EPS = 1e-5
B_HEADS = 8
B_Q_LORA = 768
B_KV_LORA = 512
B_NOPE_DIM = 128
B_ROPE_DIM = 64
B_V_DIM = 128
B_QK_DIM = B_NOPE_DIM + B_ROPE_DIM
ROPE_THETA = 10000.0
LRU_BLOCKS = 8
CONV_WIDTH = 4
LRU_C = 8.0
LOG2E = math.log2(math.e)

LANES = 128
SUBLANES = 8
VMEM_LIMIT = 56 * 1024 * 1024

C_CQ = 0
C_CKV = C_CQ + B_Q_LORA
C_KR = C_CKV + B_KV_LORA
C_QA = C_KR + LANES
C_KA = C_QA + A_HEADS * 2 * A_QK_DIM
C_VA = C_KA + A_HEADS * 2 * A_QK_DIM
C_END = C_VA + A_HEADS * A_V_DIM
MXU_N = 256
C_PAD = -(-C_END // (3 * MXU_N)) * (3 * MXU_N)


def _params(*sem):
    return pltpu.CompilerParams(dimension_semantics=sem, vmem_limit_bytes=VMEM_LIMIT)


def _tile(n, pref):
    t = min(n, pref)
    assert n % t == 0, (n, t)
    return t


def _rms(x, g, eps):
    return x * lax.rsqrt(jnp.mean(x * x, axis=-1, keepdims=True) + eps) * g


def _rope_table_kernel(pos_ref, invf_ref, o_ref):
    ang = pos_ref[...] * invf_ref[...]
    lane = lax.broadcasted_iota(jnp.int32, ang.shape, 1)
    o_ref[...] = jnp.where(lane < B_ROPE_DIM, jnp.cos(ang), jnp.sin(ang))


def _rope_table(pos_lanes, invf):
    t = pos_lanes.shape[0]
    tm = _tile(t, 1024)
    return pl.pallas_call(
        _rope_table_kernel,
        out_shape=jax.ShapeDtypeStruct((t, LANES), F32),
        grid=(t // tm,),
        in_specs=[pl.BlockSpec((tm, LANES), lambda i: (i, 0)),
                  pl.BlockSpec((1, LANES), lambda i: (0, 0))],
        out_specs=pl.BlockSpec((tm, LANES), lambda i: (i, 0)),
        compiler_params=_params("parallel"),
        name="rope_table",
    )(pos_lanes, invf)


def _norm_matmul_kernel(x_ref, g_ref, w_ref, o_ref, xn_ref):
    @pl.when(pl.program_id(1) == 0)
    def _():
        xn_ref[...] = _rms(x_ref[...], g_ref[...], NORM_EPS).astype(BF16)

    o_ref[...] = jnp.dot(xn_ref[...], w_ref[...],
                         preferred_element_type=F32).astype(o_ref.dtype)


def _norm_matmul(x, g, w, layer, out_dtype, name, tm=1024, tn=768):
    t, k = x.shape
    n = w.shape[2]
    tm = _tile(t, tm)
    tn = _tile(n, tn)
    return pl.pallas_call(
        _norm_matmul_kernel,
        out_shape=jax.ShapeDtypeStruct((t, n), out_dtype),
        grid=(t // tm, n // tn),
        in_specs=[pl.BlockSpec((tm, k), lambda i, j: (i, 0)),
                  pl.BlockSpec((1, k), lambda i, j: (0, 0)),
                  pl.BlockSpec((None, k, tn), lambda i, j: (layer, 0, j))],
        out_specs=pl.BlockSpec((tm, tn), lambda i, j: (i, j)),
        scratch_shapes=[pltpu.VMEM((tm, k), BF16)],
        compiler_params=_params("parallel", "arbitrary"),
        name=name,
    )(x, g, w)


def _matmul_res_kernel(*refs, n_lhs):
    lhs = refs[:n_lhs]
    ws = refs[n_lhs:2 * n_lhs]
    res_ref, o_ref = refs[2 * n_lhs], refs[2 * n_lhs + 1]
    acc = res_ref[...]
    for a, w in zip(lhs, ws):
        acc = acc + jnp.dot(a[...], w[...], preferred_element_type=F32)
    o_ref[...] = acc


def _matmul_res(lhs_list, w, layer, res, name, tm=1024, tn=1024):
    t, n = res.shape
    tm = _tile(t, tm)
    tn = _tile(n, tn)
    n_lhs = len(lhs_list)
    kp = lhs_list[0].shape[1]
    assert all(a.shape[1] == kp for a in lhs_list) and n_lhs * kp == w.shape[1]
    in_specs = []
    for a in lhs_list:
        in_specs.append(pl.BlockSpec((tm, kp), lambda i, j: (i, 0)))
    for part in range(n_lhs):
        in_specs.append(pl.BlockSpec(
            (None, kp, tn), functools.partial(lambda i, j, part: (layer, part, j), part=part)))
    in_specs.append(pl.BlockSpec((tm, tn), lambda i, j: (i, j)))
    return pl.pallas_call(
        functools.partial(_matmul_res_kernel, n_lhs=n_lhs),
        out_shape=jax.ShapeDtypeStruct((t, n), F32),
        grid=(t // tm, n // tn),
        in_specs=in_specs,
        out_specs=pl.BlockSpec((tm, tn), lambda i, j: (i, j)),
        compiler_params=_params("parallel", "parallel"),
        name=name,
    )(*lhs_list, *([w] * n_lhs), res)


def _mlp_kernel(h_ref, g_ref, w1_ref, w2_ref, gf_ref, o_ref, xn_ref, *, final_norm):
    j = pl.program_id(1)

    @pl.when(j == 0)
    def _():
        h = h_ref[...]
        xn_ref[...] = _rms(h, g_ref[...], NORM_EPS).astype(BF16)
        o_ref[...] = h

    u = jnp.maximum(jnp.dot(xn_ref[...], w1_ref[...], preferred_element_type=F32), 0.0)
    u = (u * u).astype(BF16)
    o_ref[...] += jnp.dot(u, w2_ref[...], preferred_element_type=F32)

    if final_norm:
        @pl.when(j == pl.num_programs(1) - 1)
        def _():
            o_ref[...] = _rms(o_ref[...], gf_ref[...], NORM_EPS)


def _mlp(h, g, w1, w2, layer, gf, final_norm, name, tm=1024, tf=512):
    t, d = h.shape
    f = w1.shape[2]
    tm = _tile(t, tm)
    tf = _tile(f, tf)
    return pl.pallas_call(
        functools.partial(_mlp_kernel, final_norm=final_norm),
        out_shape=jax.ShapeDtypeStruct((t, d), F32),
        grid=(t // tm, f // tf),
        in_specs=[pl.BlockSpec((tm, d), lambda i, j: (i, 0)),
                  pl.BlockSpec((1, d), lambda i, j: (0, 0)),
                  pl.BlockSpec((None, d, tf), lambda i, j: (layer, 0, j)),
                  pl.BlockSpec((None, tf, d), lambda i, j: (layer, j, 0)),
                  pl.BlockSpec((1, d), lambda i, j: (0, 0))],
        out_specs=pl.BlockSpec((tm, d), lambda i, j: (i, 0)),
        scratch_shapes=[pltpu.VMEM((tm, d), BF16)],
        compiler_params=_params("parallel", "arbitrary"),
        name=name,
    )(h, g, w1, w2, gf)


def _mla_prep_kernel(lat_ref, kr_ref, cs_ref, gq_ref, gkv_ref, wq_ref, wkv_ref,
                     q_ref, kv_ref):
    lat = lat_ref[...].astype(F32)
    cs = cs_ref[...]
    scale = B_QK_DIM ** -0.5 * LOG2E
    cqn = _rms(lat[:, :B_Q_LORA], gq_ref[...], NORM_EPS).astype(BF16)
    q = jnp.dot(cqn, wq_ref[...], preferred_element_type=F32)
    cs_scaled = cs * scale
    for h in range(B_HEADS):
        lo = h * 2 * LANES
        q_ref[:, lo:lo + LANES] = (q[:, lo:lo + LANES] * scale).astype(BF16)
        q_ref[:, lo + LANES:lo + 2 * LANES] = (
            q[:, lo + LANES:lo + 2 * LANES] * cs_scaled).astype(BF16)
    ckvn = _rms(lat[:, B_Q_LORA:], gkv_ref[...], NORM_EPS).astype(BF16)
    kv = jnp.dot(ckvn, wkv_ref[...], preferred_element_type=F32)
    nk = B_HEADS * B_NOPE_DIM
    kv_ref[:, :nk] = kv[:, :nk].astype(BF16)
    kv_ref[:, nk + LANES:] = kv[:, nk:].astype(BF16)
    t = kr_ref[...].astype(F32) * cs
    kv_ref[:, nk:nk + LANES] = (t + pltpu.roll(t, B_ROPE_DIM, 1)).astype(BF16)


def _mla_prep(proj, cs, gq, gkv, wq, wkv, tm=512):
    t = proj.shape[0]
    tm = _tile(t, tm)
    lat_w = B_Q_LORA + B_KV_LORA
    nq = B_HEADS * 2 * LANES
    nkv = B_HEADS * (B_NOPE_DIM + B_V_DIM) + LANES
    return pl.pallas_call(
        _mla_prep_kernel,
        out_shape=(jax.ShapeDtypeStruct((t, nq), BF16),
                   jax.ShapeDtypeStruct((t, nkv), BF16)),
        grid=(t // tm,),
        in_specs=[pl.BlockSpec((tm, lat_w), lambda i: (i, 0)),
                  pl.BlockSpec((tm, LANES), lambda i: (i, C_KR // LANES)),
                  pl.BlockSpec((tm, LANES), lambda i: (i, 0)),
                  pl.BlockSpec((1, B_Q_LORA), lambda i: (0, 0)),
                  pl.BlockSpec((1, B_KV_LORA), lambda i: (0, 0)),
                  pl.BlockSpec(wq.shape, lambda i: (0, 0)),
                  pl.BlockSpec(wkv.shape, lambda i: (0, 0))],
        out_specs=(pl.BlockSpec((tm, nq), lambda i: (i, 0)),
                   pl.BlockSpec((tm, nkv), lambda i: (i, 0))),
        compiler_params=_params("parallel"),
        name="mla_prep",
    )(proj, proj, cs, gq, gkv, wq, wkv)


def _flash_sweep(q_at, k_at, v_at, finish, s_ref, p_ref, m_ref, al_ref, acc_ref, *,
                 nq, tq, rc):
    rows = m_ref.shape[0]
    tk = tq // 2
    nl = tk // LANES

    full = [(0, rows)]
    late = [(m * tq + tk, (m + 1) * tq) for m in range(rows // tq)]

    def qk(tile, kb, slot, spans):
        k = k_at(kb)
        for a, b in spans:
            s_ref[slot, a:b, :] = lax.dot_general(
                q_at(tile, a, b), k, (((1,), (1,)), ((), ())), preferred_element_type=F32)

    def pv(kb, slot, spans):
        v = v_at(kb)
        v1 = jnp.concatenate([v, jnp.ones_like(v)], axis=1)
        for a, b in spans:
            upd = jnp.dot(p_ref[slot, a:b, :], v1, preferred_element_type=F32)
            al = al_ref[slot, a:b, :]
            for j in range(2):
                ls = slice(j * LANES, (j + 1) * LANES)
                acc_ref[a:b, ls] = al * acc_ref[a:b, ls] + upd[:, ls]

    def softmax(s_slot, slot, spans, col0):
        def hidden(r0, j):
            return col0 is not None and col0 + j * LANES > r0 % tq + rc - 1

        def load(r0, j):
            x = s_ref[s_slot, r0:r0 + rc, j * LANES:(j + 1) * LANES]
            if col0 is not None and col0 + (j + 1) * LANES - 1 > r0 % tq:
                row = lax.broadcasted_iota(jnp.int32, x.shape, 0) + (r0 % tq)
                col = lax.broadcasted_iota(jnp.int32, x.shape, 1) + (col0 + j * LANES)
                x = jnp.where(col <= row, x, NEG_INF)
            return x

        chunks = [r0 for a, b in spans for r0 in range(a, b, rc)]
        for r0 in chunks:
            live = [j for j in range(nl) if not hidden(r0, j)]
            pm = load(r0, live[0])
            for j in live[1:]:
                pm = jnp.maximum(pm, load(r0, j))
            mb = jnp.broadcast_to(jnp.max(pm, axis=-1, keepdims=True), (rc, LANES))
            m_old = m_ref[r0:r0 + rc, :]
            m_new = jnp.maximum(m_old, mb)
            al_ref[slot, r0:r0 + rc, :] = jnp.exp2(m_old - m_new)
            m_ref[r0:r0 + rc, :] = m_new

        for r0 in chunks:
            m_new = m_ref[r0:r0 + rc, :]
            for j in range(nl):
                ls = slice(j * LANES, (j + 1) * LANES)
                if hidden(r0, j):
                    p_ref[slot, r0:r0 + rc, ls] = jnp.zeros((rc, LANES), BF16)
                else:
                    p_ref[slot, r0:r0 + rc, ls] = jnp.exp2(load(r0, j) - m_new).astype(BF16)

    qk(0, 0, 2, full)

    def tile(qi, carry):
        s_ref[0] = s_ref[2]
        m_ref[...] = jnp.full(m_ref.shape, NEG_INF, F32)
        acc_ref[...] = jnp.zeros(acc_ref.shape, F32)
        p_ref[1] = jnp.zeros(p_ref.shape[1:], BF16)
        al_ref[1] = jnp.ones(al_ref.shape[1:], F32)

        def pair(tp, c):
            e = 2 * tp
            qk(qi, e + 1, 1, full)
            softmax(0, 0, full, None)
            pv(jnp.maximum(e - 1, 0), 1, full)
            qk(qi, e + 2, 0, full)
            softmax(1, 1, full, None)
            pv(e, 0, full)
            return c

        lax.fori_loop(0, qi, pair, 0)
        e = 2 * qi
        qk(qi, e + 1, 1, late)
        softmax(0, 2, full, 0)
        pv(jnp.maximum(e - 1, 0), 1, full)
        qk(jnp.minimum(qi + 1, nq - 1), 0, 2, full)
        softmax(1, 3, late, tk)
        pv(e, 2, full)
        pv(e + 1, 3, late)
        finish(qi)
        return carry

    lax.fori_loop(0, nq, tile, 0)


def _flash_scratch(rows, tq, dv):
    tk = tq // 2
    return [pltpu.VMEM((3, rows, tk), F32),
            pltpu.VMEM((4, rows, tk), BF16),
            pltpu.VMEM((rows, LANES), F32),
            pltpu.VMEM((4, rows, LANES), F32),
            pltpu.VMEM((rows, 2 * dv), F32)]


def _diff_attn_kernel(q_ref, k_ref, v_ref, lam_ref, g_ref, o_ref, q_s, *scratch,
                      nq, tq, rc, lambda_init):
    rows = 2 * tq
    tk = tq // 2
    lane = lax.broadcasted_iota(jnp.int32, (tq, LANES), 1)
    zero = jnp.zeros((tq, LANES), BF16)
    for i in range(nq):
        q = q_ref[i * tq:(i + 1) * tq, :]
        q_s[i * rows:i * rows + tq, :] = jnp.where(lane < A_QK_DIM, q, zero)
        q_s[i * rows + tq:(i + 1) * rows, :] = jnp.where(lane < A_QK_DIM, zero, q)

    def q_at(tile, a, b):
        return q_s[pl.ds(pl.multiple_of(tile * rows + a, SUBLANES * 2), b - a), :]

    def k_at(kb):
        return k_ref[pl.ds(pl.multiple_of(kb * tk, tk), tk), :]

    def v_at(kb):
        return v_ref[pl.ds(pl.multiple_of(kb * tk, tk), tk), :]

    lv = lam_ref[...]
    lam = (jnp.exp(jnp.sum(lv[0:1] * lv[1:2], axis=-1, keepdims=True))
           - jnp.exp(jnp.sum(lv[2:3] * lv[3:4], axis=-1, keepdims=True))
           + lambda_init)
    acc_ref = scratch[-1]

    def finish(qi):
        acc = acc_ref[...]
        o = acc[:, :A_V_DIM] / acc[:, A_V_DIM:]
        d = o[:tq] - lam * o[tq:]
        o_ref[pl.ds(pl.multiple_of(qi * tq, tq), tq), :] = (
            _rms(d, g_ref[...], A_SUBLN_EPS) * (1.0 - lambda_init)).astype(o_ref.dtype)

    _flash_sweep(q_at, k_at, v_at, finish, *scratch, nq=nq, tq=tq, rc=rc)


def _diff_attn(proj, lamv, subln, batch, seq, lambda_init, tq=512, rc=64):
    t = proj.shape[0]
    tq = _tile(seq, tq)
    nq = seq // tq
    qc, kc, vc = C_QA // LANES, C_KA // LANES, C_VA // LANES
    return pl.pallas_call(
        functools.partial(_diff_attn_kernel, nq=nq, tq=tq, rc=rc, lambda_init=lambda_init),
        out_shape=jax.ShapeDtypeStruct((t, A_HEADS * A_V_DIM), BF16),
        grid=(batch, A_HEADS),
        in_specs=[pl.BlockSpec((seq, LANES), lambda b, h: (b, qc + h)),
                  pl.BlockSpec((seq, LANES), lambda b, h: (b, kc + h)),
                  pl.BlockSpec((seq, LANES), lambda b, h: (b, vc + h)),
                  pl.BlockSpec(lamv.shape, lambda b, h: (0, 0)),
                  pl.BlockSpec((1, A_V_DIM), lambda b, h: (0, 0))],
        out_specs=pl.BlockSpec((seq, LANES), lambda b, h: (b, h)),
        scratch_shapes=[pltpu.VMEM((2 * seq, LANES), BF16)] + _flash_scratch(2 * tq, tq, A_V_DIM),
        compiler_params=_params("parallel", "parallel"),
        name="diff_attn",
    )(proj, proj, proj, lamv, subln)


def _mla_attn_kernel(q_ref, kn_ref, kr_ref, v_ref, o_ref, *scratch, nq, tq, rc):
    tk = tq // 2

    def q_at(tile, a, b):
        return q_ref[pl.ds(pl.multiple_of(tile * tq + a, SUBLANES * 2), b - a), :]

    def k_at(kb):
        sl = pl.ds(pl.multiple_of(kb * tk, tk), tk)
        return jnp.concatenate([kn_ref[sl, :], kr_ref[sl, :]], axis=1)

    def v_at(kb):
        return v_ref[pl.ds(pl.multiple_of(kb * tk, tk), tk), :]

    acc_ref = scratch[-1]

    def finish(qi):
        acc = acc_ref[...]
        o_ref[pl.ds(pl.multiple_of(qi * tq, tq), tq), :] = (
            acc[:, :B_V_DIM] / acc[:, B_V_DIM:]).astype(o_ref.dtype)

    _flash_sweep(q_at, k_at, v_at, finish, *scratch, nq=nq, tq=tq, rc=rc)


def _mla_attn(qb, kvb, batch, seq, tq=1024, rc=64):
    t = qb.shape[0]
    tq = _tile(seq, tq)
    nq = seq // tq
    return pl.pallas_call(
        functools.partial(_mla_attn_kernel, nq=nq, tq=tq, rc=rc),
        out_shape=jax.ShapeDtypeStruct((t, B_HEADS * B_V_DIM), BF16),
        grid=(batch, B_HEADS),
        in_specs=[pl.BlockSpec((seq, 2 * LANES), lambda b, h: (b, h)),
                  pl.BlockSpec((seq, LANES), lambda b, h: (b, h)),
                  pl.BlockSpec((seq, LANES), lambda b, h: (b, B_HEADS)),
                  pl.BlockSpec((seq, LANES), lambda b, h: (b, B_HEADS + 1 + h))],
        out_specs=pl.BlockSpec((seq, LANES), lambda b, h: (b, h)),
        scratch_shapes=_flash_scratch(tq, tq, B_V_DIM),
        compiler_params=_params("parallel", "parallel"),
        name="mla_attn",
    )(qb, kvb, kvb, kvb)


def _rglru_kernel(y_ref, x_ref, cw_ref, cb_ref, wg_ref, bg_ref, lam_ref, o_ref,
                  halo_ref, h_ref, xs_ref, hs_ref, *, ts, bw):
    seg = ts // SUBLANES
    pitch = xs_ref.shape[1] // SUBLANES
    nlc = bw // LANES

    def put(ref, r0, v):
        for c in range(nlc):
            ref[c, r0:r0 + v.shape[0], :] = v[:, c * LANES:(c + 1) * LANES]

    def take(ref, start, stride):
        return jnp.concatenate(
            [ref[c, pl.ds(start, SUBLANES, stride=stride), :] for c in range(nlc)], axis=1)

    @pl.when(pl.program_id(2) == 0)
    def _():
        halo_ref[...] = jnp.zeros_like(halo_ref)
        h_ref[...] = jnp.zeros_like(h_ref)

    x = x_ref[...].astype(F32)
    for s in range(SUBLANES):
        put(xs_ref, s * pitch, x[s * seg:(s + 1) * seg])
    xp = [take(xs_ref, g, pitch) for g in range(seg)]
    sub = lax.broadcasted_iota(jnp.int32, (SUBLANES, bw), 0)
    halo = halo_ref[...]
    before = {}
    for k in range(1, CONV_WIDTH):
        prev_tile = halo[SUBLANES - k:SUBLANES - k + 1]
        before[-k] = jnp.where(sub == 0, prev_tile, pltpu.roll(xp[seg - k], 1, 0))
    halo_ref[...] = x[ts - SUBLANES:]
    cw = cw_ref[...]
    cb = cb_ref[...]
    xc = []
    for g in range(seg):
        acc = cb + cw[CONV_WIDTH - 1:CONV_WIDTH] * xp[g]
        for k in range(1, CONV_WIDTH):
            src = xp[g - k] if g - k >= 0 else before[g - k]
            acc = acc + cw[CONV_WIDTH - 1 - k:CONV_WIDTH - k] * src
        xc.append(acc)
    xc = jnp.concatenate(xc, axis=0)

    gates = jnp.dot(xc.astype(BF16), wg_ref[...], preferred_element_type=F32) + bg_ref[...]
    r = jax.nn.sigmoid(gates[:, :bw])
    i = jax.nn.sigmoid(gates[:, bw:])
    log_a = -LRU_C * r * jax.nn.softplus(-lam_ref[...])
    a = jnp.exp(log_a)
    b = jnp.sqrt(-jnp.tanh(log_a) * (a * a + 1.0)) * (i * xc)

    hl, ap = [], []
    h = jnp.zeros((SUBLANES, bw), F32)
    prod = jnp.ones((SUBLANES, bw), F32)
    for g in range(seg):
        ag = a[g * SUBLANES:(g + 1) * SUBLANES]
        h = ag * h + b[g * SUBLANES:(g + 1) * SUBLANES]
        prod = ag * prod
        hl.append(h)
        ap.append(prod)
    c = h_ref[...]
    carry = []
    for s in range(SUBLANES):
        carry.append(c)
        c = prod[s:s + 1] * c + h[s:s + 1]
    h_ref[...] = c
    carry = jnp.concatenate(carry, axis=0)
    for g in range(seg):
        put(hs_ref, g * SUBLANES, hl[g] + ap[g] * carry)
    rows = []
    for n in range(ts // SUBLANES):
        start = ((n * SUBLANES) % seg) * SUBLANES + (n * SUBLANES) // seg
        rows.append(take(hs_ref, start, SUBLANES))
    hs = jnp.concatenate(rows, axis=0)
    y = jax.nn.gelu(y_ref[...].astype(F32), approximate=True)
    o_ref[...] = (y * hs).astype(o_ref.dtype)


def _rglru(proj, conv_w, conv_b, wg, bg, lam, batch, seq, ts=1024):
    t = proj.shape[0]
    width = proj.shape[1] // 2
    bw = width // LRU_BLOCKS
    ts = _tile(seq, ts)
    ns = seq // ts
    return pl.pallas_call(
        functools.partial(_rglru_kernel, ts=ts, bw=bw),
        out_shape=jax.ShapeDtypeStruct((t, width), BF16),
        grid=(batch, LRU_BLOCKS, ns),
        in_specs=[pl.BlockSpec((ts, bw), lambda b, n, s: (b * ns + s, n)),
                  pl.BlockSpec((ts, bw), lambda b, n, s: (b * ns + s, LRU_BLOCKS + n)),
                  pl.BlockSpec((CONV_WIDTH, bw), lambda b, n, s: (0, n)),
                  pl.BlockSpec((1, bw), lambda b, n, s: (0, n)),
                  pl.BlockSpec((None, bw, 2 * bw), lambda b, n, s: (n, 0, 0)),
                  pl.BlockSpec((None, 1, 2 * bw), lambda b, n, s: (n, 0, 0)),
                  pl.BlockSpec((1, bw), lambda b, n, s: (0, n))],
        out_specs=pl.BlockSpec((ts, bw), lambda b, n, s: (b * ns + s, n)),
        scratch_shapes=[pltpu.VMEM((SUBLANES, bw), F32), pltpu.VMEM((1, bw), F32),
                        pltpu.VMEM((bw // LANES, ts + SUBLANES * SUBLANES, LANES), F32),
                        pltpu.VMEM((bw // LANES, ts, LANES), F32)],
        compiler_params=_params("parallel", "parallel", "arbitrary"),
        name="rglru",
    )(proj, proj, conv_w, conv_b, wg, bg, lam)


def _rot_half_cols(w):
    half = w.shape[-1] // 2
    return jnp.concatenate([-w[..., half:], w[..., :half]], axis=-1)


def _prep_attn_w_in(w):
    o0 = A_HEADS * 2 * A_QK_DIM
    o1 = 2 * o0
    o2 = o1 + A_HEADS * A_V_DIM
    o3 = o2 + B_Q_LORA
    o4 = o3 + B_KV_LORA
    kr = w[:, o4:]
    return jnp.concatenate(
        [w[:, o2:o3], w[:, o3:o4], kr, _rot_half_cols(kr),
         w[:, :o0] * (A_QK_DIM ** -0.5 * LOG2E), w[:, o0:o1], w[:, o1:o2],
         jnp.zeros((w.shape[0], C_PAD - C_END), w.dtype)], axis=1).astype(BF16)


def _prep_w_uq(w):
    k = w.shape[0]
    w = w.reshape(k, B_HEADS, B_QK_DIM)
    rope = w[:, :, B_NOPE_DIM:]
    return jnp.concatenate([w[:, :, :B_NOPE_DIM], rope, _rot_half_cols(rope)],
                           axis=-1).reshape(k, B_HEADS * 2 * LANES).astype(BF16)


def _prep_w_ukv(w):
    k = w.shape[0]
    w = w.reshape(k, B_HEADS, B_NOPE_DIM + B_V_DIM)
    return jnp.concatenate([w[:, :, :B_NOPE_DIM].reshape(k, -1),
                            w[:, :, B_NOPE_DIM:].reshape(k, -1)], axis=1).astype(BF16)


def kernel(x, positions, norm_mix, norm_mlp, norm_final, attn_w_in, attn_lambda_q1,
           attn_lambda_k1, attn_lambda_q2, attn_lambda_k2, attn_subln, attn_q_norm,
           attn_kv_norm, attn_w_uq, attn_w_ukv, attn_w_out, rec_w_in, rec_conv_w,
           rec_conv_b, rec_w_a, rec_b_a, rec_w_x, rec_b_x, rec_lambda, rec_w_out,
           mlp_w1, mlp_w2):
    batch, seq, d = x.shape
    t = batch * seq
    h0 = x.reshape(t, d)
    row = lambda v: v.reshape(1, -1)

    half = B_ROPE_DIM // 2
    inv_freq = 1.0 / (ROPE_THETA ** (jnp.arange(0, B_ROPE_DIM, 2, dtype=F32) / B_ROPE_DIM))
    invf = jnp.tile(inv_freq, LANES // half).reshape(1, LANES)
    pos_lanes = jnp.broadcast_to(positions.astype(F32).reshape(t, 1), (t, LANES))
    cs = _rope_table(pos_lanes, invf)

    proj = _norm_matmul(h0, row(norm_mix[0]), _prep_attn_w_in(attn_w_in[0])[None], 0, BF16,
                        "attn_in_proj")
    qb, kvb = _mla_prep(proj, cs, row(attn_q_norm[0]), row(attn_kv_norm[0]),
                        _prep_w_uq(attn_w_uq[0]), _prep_w_ukv(attn_w_ukv[0]))
    lambda_init = 0.8 - 0.6 * math.exp(-0.3 * 0)
    lamv = jnp.stack([attn_lambda_q1[0], attn_lambda_k1[0],
                      attn_lambda_q2[0], attn_lambda_k2[0]]).astype(F32)
    oa = _diff_attn(proj, lamv, row(attn_subln[0]), batch, seq, lambda_init)
    ob = _mla_attn(qb, kvb, batch, seq)
    h1 = _matmul_res([oa, ob], attn_w_out.astype(BF16), 0, h0, "attn_out_proj")
    w1 = mlp_w1.astype(BF16)
    w2 = mlp_w2.astype(BF16)
    h2 = _mlp(h1, row(norm_mlp[0]), w1, w2, 0, row(norm_final), False, "mlp0")

    proj2 = _norm_matmul(h2, row(norm_mix[1]), rec_w_in.astype(BF16), 0, BF16,
                         "rec_in_proj", tn=1024)
    wg = jnp.concatenate([rec_w_a[0], rec_w_x[0]], axis=-1).astype(BF16)
    bg = jnp.concatenate([rec_b_a[0], rec_b_x[0]], axis=-1)[:, None, :]
    g = _rglru(proj2, rec_conv_w[0], row(rec_conv_b[0]), wg, bg, row(rec_lambda[0]),
               batch, seq)
    h3 = _matmul_res([g], rec_w_out.astype(BF16), 0, h2, "rec_out_proj")
    out = _mlp(h3, row(norm_mlp[1]), w1, w2, 1, row(norm_final), True, "mlp1")
    return out.reshape(batch, seq, d)
```

```python
import functools
import math

import jax
import jax.numpy as jnp
from jax import lax
from jax.experimental import pallas as pl
from jax.experimental.pallas import tpu as pltpu

F32 = jnp.float32
BF16 = jnp.bfloat16

NORM_EPS = 1e-6
NEG_INF = -1e30
A_HEADS = 8
A_QK_DIM = 64
A_V_DIM = 128
A_SUBLN_EPS = 1e-5
B_HEADS = 8
B_Q_LORA = 768
B_KV_LORA = 512
B_NOPE_DIM = 128
B_ROPE_DIM = 64
B_V_DIM = 128
B_QK_DIM = B_NOPE_DIM + B_ROPE_DIM
ROPE_THETA = 10000.0
LRU_BLOCKS = 8
CONV_WIDTH = 4
LRU_C = 8.0
LOG2E = math.log2(math.e)

LANES = 128
SUBLANES = 8
VMEM_LIMIT = 56 * 1024 * 1024

MXU_N = 256
C_QA = 0
C_KA = C_QA + A_HEADS * 2 * A_QK_DIM
C_VA = C_KA + A_HEADS * 2 * A_QK_DIM
C_CQ = C_VA + A_HEADS * A_V_DIM
C_CKV = C_CQ + B_Q_LORA
C_KR = C_CKV + B_KV_LORA
C_END = C_KR + 2 * B_ROPE_DIM
C_PAD = -(-C_END // (3 * MXU_N)) * (3 * MXU_N)
CKV_BLK = 256


def _params(*sem):
    return pltpu.CompilerParams(dimension_semantics=sem, vmem_limit_bytes=VMEM_LIMIT)


def _tile(n, pref):
    t = min(n, pref)
    assert n % t == 0, (n, t)
    return t


def _rms(x, g, eps):
    return x * lax.rsqrt(jnp.mean(x * x, axis=-1, keepdims=True) + eps) * g


def _rope_table_kernel(pos_ref, invf_ref, o_ref):
    ang = pos_ref[...] * invf_ref[...]
    lane = lax.broadcasted_iota(jnp.int32, ang.shape, 1)
    o_ref[...] = jnp.where(lane < B_ROPE_DIM, jnp.cos(ang), jnp.sin(ang))


def _rope_table(pos_lanes, invf):
    t = pos_lanes.shape[0]
    tm = _tile(t, 1024)
    return pl.pallas_call(
        _rope_table_kernel,
        out_shape=jax.ShapeDtypeStruct((t, LANES), F32),
        grid=(t // tm,),
        in_specs=[pl.BlockSpec((tm, LANES), lambda i: (i, 0)),
                  pl.BlockSpec((1, LANES), lambda i: (0, 0))],
        out_specs=pl.BlockSpec((tm, LANES), lambda i: (i, 0)),
        compiler_params=_params("parallel"),
        name="rope_table",
    )(pos_lanes, invf)


def _norm_matmul_kernel(x_ref, g_ref, w_ref, o_ref, xn_ref):
    @pl.when(pl.program_id(1) == 0)
    def _():
        xn_ref[...] = _rms(x_ref[...], g_ref[...], NORM_EPS).astype(BF16)

    o_ref[...] = jnp.dot(xn_ref[...], w_ref[...],
                         preferred_element_type=F32).astype(o_ref.dtype)


def _norm_matmul(x, g, w, layer, out_dtype, name, tm=1024, tn=768):
    t, k = x.shape
    n = w.shape[2]
    tm = _tile(t, tm)
    tn = _tile(n, tn)
    return pl.pallas_call(
        _norm_matmul_kernel,
        out_shape=jax.ShapeDtypeStruct((t, n), out_dtype),
        grid=(t // tm, n // tn),
        in_specs=[pl.BlockSpec((tm, k), lambda i, j: (i, 0)),
                  pl.BlockSpec((1, k), lambda i, j: (0, 0)),
                  pl.BlockSpec((None, k, tn), lambda i, j: (layer, 0, j))],
        out_specs=pl.BlockSpec((tm, tn), lambda i, j: (i, j)),
        scratch_shapes=[pltpu.VMEM((tm, k), BF16)],
        compiler_params=_params("parallel", "arbitrary"),
        name=name,
    )(x, g, w)


def _matmul_res_kernel(*refs, n_lhs):
    lhs = refs[:n_lhs]
    ws = refs[n_lhs:2 * n_lhs]
    res_ref, o_ref = refs[2 * n_lhs], refs[2 * n_lhs + 1]
    acc = res_ref[...]
    for a, w in zip(lhs, ws):
        acc = acc + jnp.dot(a[...], w[...], preferred_element_type=F32)
    o_ref[...] = acc


def _matmul_res(lhs_list, w, layer, res, name, tm=512, tn=2048):
    t, n = res.shape
    tm = _tile(t, tm)
    tn = _tile(n, tn)
    n_lhs = len(lhs_list)
    kp = lhs_list[0].shape[1]
    assert all(a.shape[1] == kp for a in lhs_list) and n_lhs * kp == w.shape[1]
    in_specs = []
    for a in lhs_list:
        in_specs.append(pl.BlockSpec((tm, kp), lambda i, j: (i, 0)))
    for part in range(n_lhs):
        in_specs.append(pl.BlockSpec(
            (None, kp, tn), functools.partial(lambda i, j, part: (layer, part, j), part=part)))
    in_specs.append(pl.BlockSpec((tm, tn), lambda i, j: (i, j)))
    return pl.pallas_call(
        functools.partial(_matmul_res_kernel, n_lhs=n_lhs),
        out_shape=jax.ShapeDtypeStruct((t, n), F32),
        grid=(t // tm, n // tn),
        in_specs=in_specs,
        out_specs=pl.BlockSpec((tm, tn), lambda i, j: (i, j)),
        compiler_params=_params("parallel", "parallel"),
        name=name,
    )(*lhs_list, *([w] * n_lhs), res)


def _mlp_kernel(h_ref, g_ref, w1_ref, w2_ref, gf_ref, o_ref, xn_ref, *, final_norm):
    j = pl.program_id(1)

    @pl.when(j == 0)
    def _():
        h = h_ref[...]
        xn_ref[...] = _rms(h, g_ref[...], NORM_EPS).astype(BF16)
        o_ref[...] = h

    u = jnp.maximum(jnp.dot(xn_ref[...], w1_ref[...], preferred_element_type=F32), 0.0)
    u = (u * u).astype(BF16)
    o_ref[...] += jnp.dot(u, w2_ref[...], preferred_element_type=F32)

    if final_norm:
        @pl.when(j == pl.num_programs(1) - 1)
        def _():
            o_ref[...] = _rms(o_ref[...], gf_ref[...], NORM_EPS)


def _mlp(h, g, w1, w2, layer, gf, final_norm, name, tm=1024, tf=512):
    t, d = h.shape
    f = w1.shape[2]
    tm = _tile(t, tm)
    tf = _tile(f, tf)
    return pl.pallas_call(
        functools.partial(_mlp_kernel, final_norm=final_norm),
        out_shape=jax.ShapeDtypeStruct((t, d), F32),
        grid=(t // tm, f // tf),
        in_specs=[pl.BlockSpec((tm, d), lambda i, j: (i, 0)),
                  pl.BlockSpec((1, d), lambda i, j: (0, 0)),
                  pl.BlockSpec((None, d, tf), lambda i, j: (layer, 0, j)),
                  pl.BlockSpec((None, tf, d), lambda i, j: (layer, j, 0)),
                  pl.BlockSpec((1, d), lambda i, j: (0, 0))],
        out_specs=pl.BlockSpec((tm, d), lambda i, j: (i, 0)),
        scratch_shapes=[pltpu.VMEM((tm, d), BF16)],
        compiler_params=_params("parallel", "arbitrary"),
        name=name,
    )(h, g, w1, w2, gf)


def _mla_prep_kernel(cq_ref, ckv0_ref, ckv1_ref, kr_ref, cs_ref, gq_ref, gkv_ref, wq_ref,
                     wkv_ref, q_ref, kv_ref):
    cq = cq_ref[...].astype(F32)
    ckv = jnp.concatenate([ckv0_ref[...], ckv1_ref[...]], axis=1).astype(F32)
    cs = cs_ref[...]
    scale = B_QK_DIM ** -0.5 * LOG2E
    cqn = _rms(cq, gq_ref[...], NORM_EPS).astype(BF16)
    q = jnp.dot(cqn, wq_ref[...], preferred_element_type=F32)
    cs_scaled = cs * scale
    for h in range(B_HEADS):
        lo = h * 2 * LANES
        q_ref[:, lo:lo + LANES] = (q[:, lo:lo + LANES] * scale).astype(BF16)
        q_ref[:, lo + LANES:lo + 2 * LANES] = (
            q[:, lo + LANES:lo + 2 * LANES] * cs_scaled).astype(BF16)
    ckvn = _rms(ckv, gkv_ref[...], NORM_EPS).astype(BF16)
    kv = jnp.dot(ckvn, wkv_ref[...], preferred_element_type=F32)
    nk = B_HEADS * B_NOPE_DIM
    kv_ref[:, :nk] = kv[:, :nk].astype(BF16)
    kv_ref[:, nk + LANES:] = kv[:, nk:].astype(BF16)
    t = kr_ref[...].astype(F32) * cs
    kv_ref[:, nk:nk + LANES] = (t + pltpu.roll(t, B_ROPE_DIM, 1)).astype(BF16)


def _mla_prep(proj, cs, gq, gkv, wq, wkv, tm=512):
    t = proj.shape[0]
    tm = _tile(t, tm)
    assert C_CQ % B_Q_LORA == 0 and C_CKV % CKV_BLK == 0 and B_KV_LORA == 2 * CKV_BLK
    nq = B_HEADS * 2 * LANES
    nkv = B_HEADS * (B_NOPE_DIM + B_V_DIM) + LANES
    return pl.pallas_call(
        _mla_prep_kernel,
        out_shape=(jax.ShapeDtypeStruct((t, nq), BF16),
                   jax.ShapeDtypeStruct((t, nkv), BF16)),
        grid=(t // tm,),
        in_specs=[pl.BlockSpec((tm, B_Q_LORA), lambda i: (i, C_CQ // B_Q_LORA)),
                  pl.BlockSpec((tm, CKV_BLK), lambda i: (i, C_CKV // CKV_BLK)),
                  pl.BlockSpec((tm, CKV_BLK), lambda i: (i, C_CKV // CKV_BLK + 1)),
                  pl.BlockSpec((tm, LANES), lambda i: (i, C_KR // LANES)),
                  pl.BlockSpec((tm, LANES), lambda i: (i, 0)),
                  pl.BlockSpec((1, B_Q_LORA), lambda i: (0, 0)),
                  pl.BlockSpec((1, B_KV_LORA), lambda i: (0, 0)),
                  pl.BlockSpec(wq.shape, lambda i: (0, 0)),
                  pl.BlockSpec(wkv.shape, lambda i: (0, 0))],
        out_specs=(pl.BlockSpec((tm, nq), lambda i: (i, 0)),
                   pl.BlockSpec((tm, nkv), lambda i: (i, 0))),
        compiler_params=_params("parallel"),
        name="mla_prep",
    )(proj, proj, proj, proj, cs, gq, gkv, wq, wkv)


def _flash_sweep(q_at, k_at, v_at, finish, s_ref, p_ref, m_ref, al_ref, acc_ref, *,
                 nq, tq, rc):
    rows = m_ref.shape[0]
    tk = tq // 2
    nl = tk // LANES

    full = [(0, rows)]
    late = [(m * tq + tk, (m + 1) * tq) for m in range(rows // tq)]

    def qk(tile, kb, slot, spans):
        k = k_at(kb)
        for a, b in spans:
            s_ref[slot, a:b, :] = lax.dot_general(
                q_at(tile, a, b), k, (((1,), (1,)), ((), ())), preferred_element_type=F32)

    def pv(kb, slot, spans):
        v = v_at(kb)
        v1 = jnp.concatenate([v, jnp.ones_like(v)], axis=1)
        for a, b in spans:
            upd = jnp.dot(p_ref[slot, a:b, :], v1, preferred_element_type=F32)
            al = al_ref[slot, a:b, :]
            for j in range(2):
                ls = slice(j * LANES, (j + 1) * LANES)
                acc_ref[a:b, ls] = al * acc_ref[a:b, ls] + upd[:, ls]

    def softmax(s_slot, slot, spans, col0):
        def hidden(r0, j):
            return col0 is not None and col0 + j * LANES > r0 % tq + rc - 1

        def load(r0, j):
            x = s_ref[s_slot, r0:r0 + rc, j * LANES:(j + 1) * LANES]
            if col0 is not None and col0 + (j + 1) * LANES - 1 > r0 % tq:
                row = lax.broadcasted_iota(jnp.int32, x.shape, 0) + (r0 % tq)
                col = lax.broadcasted_iota(jnp.int32, x.shape, 1) + (col0 + j * LANES)
                x = jnp.where(col <= row, x, NEG_INF)
            return x

        chunks = [r0 for a, b in spans for r0 in range(a, b, rc)]
        for r0 in chunks:
            live = [j for j in range(nl) if not hidden(r0, j)]
            pm = load(r0, live[0])
            for j in live[1:]:
                pm = jnp.maximum(pm, load(r0, j))
            mb = jnp.broadcast_to(jnp.max(pm, axis=-1, keepdims=True), (rc, LANES))
            m_old = m_ref[r0:r0 + rc, :]
            m_new = jnp.maximum(m_old, mb)
            al_ref[slot, r0:r0 + rc, :] = jnp.exp2(m_old - m_new)
            m_ref[r0:r0 + rc, :] = m_new

        for r0 in chunks:
            m_new = m_ref[r0:r0 + rc, :]
            for j in range(nl):
                ls = slice(j * LANES, (j + 1) * LANES)
                if hidden(r0, j):
                    p_ref[slot, r0:r0 + rc, ls] = jnp.zeros((rc, LANES), BF16)
                else:
                    p_ref[slot, r0:r0 + rc, ls] = jnp.exp2(load(r0, j) - m_new).astype(BF16)

    qk(0, 0, 2, full)

    def tile(qi, carry):
        s_ref[0] = s_ref[2]
        m_ref[...] = jnp.full(m_ref.shape, NEG_INF, F32)
        acc_ref[...] = jnp.zeros(acc_ref.shape, F32)
        p_ref[1] = jnp.zeros(p_ref.shape[1:], BF16)
        al_ref[1] = jnp.ones(al_ref.shape[1:], F32)

        def pair(tp, c):
            e = 2 * tp
            qk(qi, e + 1, 1, full)
            softmax(0, 0, full, None)
            pv(jnp.maximum(e - 1, 0), 1, full)
            qk(qi, e + 2, 0, full)
            softmax(1, 1, full, None)
            pv(e, 0, full)
            return c

        lax.fori_loop(0, qi, pair, 0)
        e = 2 * qi
        qk(qi, e + 1, 1, late)
        softmax(0, 2, full, 0)
        pv(jnp.maximum(e - 1, 0), 1, full)
        qk(jnp.minimum(qi + 1, nq - 1), 0, 2, full)
        softmax(1, 3, late, tk)
        pv(e, 2, full)
        pv(e + 1, 3, late)
        finish(qi)
        return carry

    lax.fori_loop(0, nq, tile, 0)


def _flash_scratch(rows, tq, dv):
    tk = tq // 2
    return [pltpu.VMEM((3, rows, tk), F32),
            pltpu.VMEM((4, rows, tk), BF16),
            pltpu.VMEM((rows, LANES), F32),
            pltpu.VMEM((4, rows, LANES), F32),
            pltpu.VMEM((rows, 2 * dv), F32)]


def _diff_attn_kernel(q_ref, k_ref, v_ref, lam_ref, g_ref, o_ref, q_s, *scratch,
                      nq, tq, rc, lambda_init):
    rows = 2 * tq
    tk = tq // 2
    lane = lax.broadcasted_iota(jnp.int32, (tq, LANES), 1)
    zero = jnp.zeros((tq, LANES), BF16)
    for i in range(nq):
        q = q_ref[i * tq:(i + 1) * tq, :]
        q_s[i * rows:i * rows + tq, :] = jnp.where(lane < A_QK_DIM, q, zero)
        q_s[i * rows + tq:(i + 1) * rows, :] = jnp.where(lane < A_QK_DIM, zero, q)

    def q_at(tile, a, b):
        return q_s[pl.ds(pl.multiple_of(tile * rows + a, SUBLANES * 2), b - a), :]

    def k_at(kb):
        return k_ref[pl.ds(pl.multiple_of(kb * tk, tk), tk), :]

    def v_at(kb):
        return v_ref[pl.ds(pl.multiple_of(kb * tk, tk), tk), :]

    lv = lam_ref[...]
    lam = (jnp.exp(jnp.sum(lv[0:1] * lv[1:2], axis=-1, keepdims=True))
           - jnp.exp(jnp.sum(lv[2:3] * lv[3:4], axis=-1, keepdims=True))
           + lambda_init)
    acc_ref = scratch[-1]

    def finish(qi):
        acc = acc_ref[...]
        o = acc[:, :A_V_DIM] / acc[:, A_V_DIM:]
        d = o[:tq] - lam * o[tq:]
        o_ref[pl.ds(pl.multiple_of(qi * tq, tq), tq), :] = (
            _rms(d, g_ref[...], A_SUBLN_EPS) * (1.0 - lambda_init)).astype(o_ref.dtype)

    _flash_sweep(q_at, k_at, v_at, finish, *scratch, nq=nq, tq=tq, rc=rc)


def _diff_attn(proj, lamv, subln, batch, seq, lambda_init, tq=512, rc=64):
    t = proj.shape[0]
    tq = _tile(seq, tq)
    nq = seq // tq
    qc, kc, vc = C_QA // LANES, C_KA // LANES, C_VA // LANES
    return pl.pallas_call(
        functools.partial(_diff_attn_kernel, nq=nq, tq=tq, rc=rc, lambda_init=lambda_init),
        out_shape=jax.ShapeDtypeStruct((t, A_HEADS * A_V_DIM), BF16),
        grid=(batch, A_HEADS),
        in_specs=[pl.BlockSpec((seq, LANES), lambda b, h: (b, qc + h)),
                  pl.BlockSpec((seq, LANES), lambda b, h: (b, kc + h)),
                  pl.BlockSpec((seq, LANES), lambda b, h: (b, vc + h)),
                  pl.BlockSpec(lamv.shape, lambda b, h: (0, 0)),
                  pl.BlockSpec((1, A_V_DIM), lambda b, h: (0, 0))],
        out_specs=pl.BlockSpec((seq, LANES), lambda b, h: (b, h)),
        scratch_shapes=[pltpu.VMEM((2 * seq, LANES), BF16)] + _flash_scratch(2 * tq, tq, A_V_DIM),
        compiler_params=_params("parallel", "parallel"),
        name="diff_attn",
    )(proj, proj, proj, lamv, subln)


def _mla_attn_kernel(q_ref, kn_ref, kr_ref, v_ref, o_ref, *scratch, nq, tq, rc):
    tk = tq // 2

    def q_at(tile, a, b):
        return q_ref[pl.ds(pl.multiple_of(tile * tq + a, SUBLANES * 2), b - a), :]

    def k_at(kb):
        sl = pl.ds(pl.multiple_of(kb * tk, tk), tk)
        return jnp.concatenate([kn_ref[sl, :], kr_ref[sl, :]], axis=1)

    def v_at(kb):
        return v_ref[pl.ds(pl.multiple_of(kb * tk, tk), tk), :]

    acc_ref = scratch[-1]

    def finish(qi):
        acc = acc_ref[...]
        o_ref[pl.ds(pl.multiple_of(qi * tq, tq), tq), :] = (
            acc[:, :B_V_DIM] / acc[:, B_V_DIM:]).astype(o_ref.dtype)

    _flash_sweep(q_at, k_at, v_at, finish, *scratch, nq=nq, tq=tq, rc=rc)


def _mla_attn(qb, kvb, batch, seq, tq=1024, rc=64):
    t = qb.shape[0]
    tq = _tile(seq, tq)
    nq = seq // tq
    return pl.pallas_call(
        functools.partial(_mla_attn_kernel, nq=nq, tq=tq, rc=rc),
        out_shape=jax.ShapeDtypeStruct((t, B_HEADS * B_V_DIM), BF16),
        grid=(batch, B_HEADS),
        in_specs=[pl.BlockSpec((seq, 2 * LANES), lambda b, h: (b, h)),
                  pl.BlockSpec((seq, LANES), lambda b, h: (b, h)),
                  pl.BlockSpec((seq, LANES), lambda b, h: (b, B_HEADS)),
                  pl.BlockSpec((seq, LANES), lambda b, h: (b, B_HEADS + 1 + h))],
        out_specs=pl.BlockSpec((seq, LANES), lambda b, h: (b, h)),
        scratch_shapes=_flash_scratch(tq, tq, B_V_DIM),
        compiler_params=_params("parallel", "parallel"),
        name="mla_attn",
    )(qb, kvb, kvb, kvb)


def _rglru_kernel(y_ref, x_ref, cw_ref, cb_ref, wg_ref, bg_ref, lam_ref, o_ref,
                  halo_ref, h_ref, xs_ref, hs_ref, *, ts, bw):
    seg = ts // SUBLANES
    pitch = xs_ref.shape[1] // SUBLANES
    nlc = bw // LANES

    def put(ref, r0, v):
        for c in range(nlc):
            ref[c, r0:r0 + v.shape[0], :] = v[:, c * LANES:(c + 1) * LANES]

    def take(ref, start, stride):
        return jnp.concatenate(
            [ref[c, pl.ds(start, SUBLANES, stride=stride), :] for c in range(nlc)], axis=1)

    @pl.when(pl.program_id(2) == 0)
    def _():
        halo_ref[...] = jnp.zeros_like(halo_ref)
        h_ref[...] = jnp.zeros_like(h_ref)

    x = x_ref[...].astype(F32)
    for s in range(SUBLANES):
        put(xs_ref, s * pitch, x[s * seg:(s + 1) * seg])
    xp = [take(xs_ref, g, pitch) for g in range(seg)]
    sub = lax.broadcasted_iota(jnp.int32, (SUBLANES, bw), 0)
    halo = halo_ref[...]
    before = {}
    for k in range(1, CONV_WIDTH):
        prev_tile = halo[SUBLANES - k:SUBLANES - k + 1]
        before[-k] = jnp.where(sub == 0, prev_tile, pltpu.roll(xp[seg - k], 1, 0))
    halo_ref[...] = x[ts - SUBLANES:]
    cw = cw_ref[...]
    cb = cb_ref[...]
    xc = []
    for g in range(seg):
        acc = cb + cw[CONV_WIDTH - 1:CONV_WIDTH] * xp[g]
        for k in range(1, CONV_WIDTH):
            src = xp[g - k] if g - k >= 0 else before[g - k]
            acc = acc + cw[CONV_WIDTH - 1 - k:CONV_WIDTH - k] * src
        xc.append(acc)
    xc = jnp.concatenate(xc, axis=0)

    gates = jnp.dot(xc.astype(BF16), wg_ref[...], preferred_element_type=F32) + bg_ref[...]
    r = jax.nn.sigmoid(gates[:, :bw])
    i = jax.nn.sigmoid(gates[:, bw:])
    decay = -LRU_C * jax.nn.softplus(-lam_ref[...])
    log_a = r * decay
    a = jnp.exp2(r * (decay * LOG2E))
    b = jnp.sqrt(-jnp.tanh(log_a) * (a * a + 1.0)) * (i * xc)

    hl, ap = [], []
    h = jnp.zeros((SUBLANES, bw), F32)
    prod = jnp.ones((SUBLANES, bw), F32)
    for g in range(seg):
        ag = a[g * SUBLANES:(g + 1) * SUBLANES]
        h = ag * h + b[g * SUBLANES:(g + 1) * SUBLANES]
        prod = ag * prod
        hl.append(h)
        ap.append(prod)
    c = h_ref[...]
    carry = []
    for s in range(SUBLANES):
        carry.append(c)
        c = prod[s:s + 1] * c + h[s:s + 1]
    h_ref[...] = c
    carry = jnp.concatenate(carry, axis=0)
    for g in range(seg):
        put(hs_ref, g * SUBLANES, hl[g] + ap[g] * carry)
    rows = []
    for n in range(ts // SUBLANES):
        start = ((n * SUBLANES) % seg) * SUBLANES + (n * SUBLANES) // seg
        rows.append(take(hs_ref, start, SUBLANES))
    hs = jnp.concatenate(rows, axis=0)
    y = jax.nn.gelu(y_ref[...].astype(F32), approximate=True)
    o_ref[...] = (y * hs).astype(o_ref.dtype)


def _rglru(proj, conv_w, conv_b, wg, bg, lam, batch, seq, ts=1024):
    t = proj.shape[0]
    width = proj.shape[1] // 2
    bw = width // LRU_BLOCKS
    ts = _tile(seq, ts)
    ns = seq // ts
    return pl.pallas_call(
        functools.partial(_rglru_kernel, ts=ts, bw=bw),
        out_shape=jax.ShapeDtypeStruct((t, width), BF16),
        grid=(batch, LRU_BLOCKS, ns),
        in_specs=[pl.BlockSpec((ts, bw), lambda b, n, s: (b * ns + s, n)),
                  pl.BlockSpec((ts, bw), lambda b, n, s: (b * ns + s, LRU_BLOCKS + n)),
                  pl.BlockSpec((CONV_WIDTH, bw), lambda b, n, s: (0, n)),
                  pl.BlockSpec((1, bw), lambda b, n, s: (0, n)),
                  pl.BlockSpec((None, bw, 2 * bw), lambda b, n, s: (n, 0, 0)),
                  pl.BlockSpec((None, 1, 2 * bw), lambda b, n, s: (n, 0, 0)),
                  pl.BlockSpec((1, bw), lambda b, n, s: (0, n))],
        out_specs=pl.BlockSpec((ts, bw), lambda b, n, s: (b * ns + s, n)),
        scratch_shapes=[pltpu.VMEM((SUBLANES, bw), F32), pltpu.VMEM((1, bw), F32),
                        pltpu.VMEM((bw // LANES, ts + SUBLANES * SUBLANES, LANES), F32),
                        pltpu.VMEM((bw // LANES, ts, LANES), F32)],
        compiler_params=_params("parallel", "parallel", "arbitrary"),
        name="rglru",
    )(proj, proj, conv_w, conv_b, wg, bg, lam)


def _rot_half_cols(w):
    half = w.shape[-1] // 2
    return jnp.concatenate([-w[..., half:], w[..., :half]], axis=-1)


def _prep_attn_w_in(w):
    k, n = w.shape
    qa_cols = A_HEADS * 2 * A_QK_DIM
    col_scale = jnp.where(jnp.arange(n) < qa_cols, A_QK_DIM ** -0.5 * LOG2E, 1.0).astype(w.dtype)
    kr = w[:, C_KR:]
    return jnp.concatenate(
        [(w * col_scale).astype(BF16), _rot_half_cols(kr).astype(BF16),
         jnp.zeros((k, C_PAD - C_END), BF16)], axis=1)


def _prep_w_uq(w):
    k = w.shape[0]
    w = w.reshape(k, B_HEADS, B_QK_DIM)
    rope = w[:, :, B_NOPE_DIM:]
    return jnp.concatenate([w[:, :, :B_NOPE_DIM], rope, _rot_half_cols(rope)],
                           axis=-1).reshape(k, B_HEADS * 2 * LANES).astype(BF16)


def _prep_w_ukv(w):
    k = w.shape[0]
    w = w.reshape(k, B_HEADS, B_NOPE_DIM + B_V_DIM)
    return jnp.concatenate([w[:, :, :B_NOPE_DIM].reshape(k, -1),
                            w[:, :, B_NOPE_DIM:].reshape(k, -1)], axis=1).astype(BF16)


def kernel(x, positions, norm_mix, norm_mlp, norm_final, attn_w_in, attn_lambda_q1,
           attn_lambda_k1, attn_lambda_q2, attn_lambda_k2, attn_subln, attn_q_norm,
           attn_kv_norm, attn_w_uq, attn_w_ukv, attn_w_out, rec_w_in, rec_conv_w,
           rec_conv_b, rec_w_a, rec_b_a, rec_w_x, rec_b_x, rec_lambda, rec_w_out,
           mlp_w1, mlp_w2):
    batch, seq, d = x.shape
    t = batch * seq
    h0 = x.reshape(t, d)
    row = lambda v: v.reshape(1, -1)

    half = B_ROPE_DIM // 2
    inv_freq = 1.0 / (ROPE_THETA ** (jnp.arange(0, B_ROPE_DIM, 2, dtype=F32) / B_ROPE_DIM))
    invf = jnp.tile(inv_freq, LANES // half).reshape(1, LANES)
    pos_lanes = jnp.broadcast_to(positions.astype(F32).reshape(t, 1), (t, LANES))
    cs = _rope_table(pos_lanes, invf)

    proj = _norm_matmul(h0, row(norm_mix[0]), _prep_attn_w_in(attn_w_in[0])[None], 0, BF16,
                        "attn_in_proj")
    qb, kvb = _mla_prep(proj, cs, row(attn_q_norm[0]), row(attn_kv_norm[0]),
                        _prep_w_uq(attn_w_uq[0]), _prep_w_ukv(attn_w_ukv[0]))
    lambda_init = 0.8 - 0.6 * math.exp(-0.3 * 0)
    lamv = jnp.stack([attn_lambda_q1[0], attn_lambda_k1[0],
                      attn_lambda_q2[0], attn_lambda_k2[0]]).astype(F32)
    oa = _diff_attn(proj, lamv, row(attn_subln[0]), batch, seq, lambda_init)
    ob = _mla_attn(qb, kvb, batch, seq)
    h1 = _matmul_res([oa, ob], attn_w_out.astype(BF16), 0, h0, "attn_out_proj")
    w1 = mlp_w1.astype(BF16)
    w2 = mlp_w2.astype(BF16)
    h2 = _mlp(h1, row(norm_mlp[0]), w1, w2, 0, row(norm_final), False, "mlp0")

    proj2 = _norm_matmul(h2, row(norm_mix[1]), rec_w_in.astype(BF16), 0, BF16,
                         "rec_in_proj", tn=1024)
    wg = jnp.concatenate([rec_w_a[0], rec_w_x[0]], axis=-1).astype(BF16)
    bg = jnp.concatenate([rec_b_a[0], rec_b_x[0]], axis=-1)[:, None, :]
    g = _rglru(proj2, rec_conv_w[0], row(rec_conv_b[0]), wg, bg, row(rec_lambda[0]),
               batch, seq)
    h3 = _matmul_res([g], rec_w_out.astype(BF16), 0, h2, "rec_out_proj")
    out = _mlp(h3, row(norm_mlp[1]), w1, w2, 1, row(norm_final), True, "mlp1")
    return out.reshape(batch, seq, d)
```

```python
import functools
import math

import jax
import jax.numpy as jnp
from jax import lax
from jax.experimental import pallas as pl
from jax.experimental.pallas import tpu as pltpu

F32 = jnp.float32
BF16 = jnp.bfloat16

NORM_EPS = 1e-6
NEG_INF = -1e30
A_HEADS = 8
A_QK_DIM = 64
A_V_DIM = 128
A_SUBLN_EPS = 1e-5
B_HEADS = 8
B_Q_LORA = 768
B_KV_LORA = 512
B_NOPE_DIM = 128
B_ROPE_DIM = 64
B_V_DIM = 128
B_QK_DIM = B_NOPE_DIM + B_ROPE_DIM
ROPE_THETA = 10000.0
LRU_BLOCKS = 8
CONV_WIDTH = 4
LRU_C = 8.0
LOG2E = math.log2(math.e)

LANES = 128
SUBLANES = 8
VMEM_LIMIT = 56 * 1024 * 1024

MXU_N = 256
C_QA = 0
C_KA = C_QA + A_HEADS * 2 * A_QK_DIM
C_VA = C_KA + A_HEADS * 2 * A_QK_DIM
C_CQ = C_VA + A_HEADS * A_V_DIM
C_CKV = C_CQ + B_Q_LORA
C_KR = C_CKV + B_KV_LORA
C_END = C_KR + 2 * B_ROPE_DIM
C_PAD = -(-C_END // (3 * MXU_N)) * (3 * MXU_N)
CKV_BLK = 256


def _params(*sem):
    return pltpu.CompilerParams(dimension_semantics=sem, vmem_limit_bytes=VMEM_LIMIT)


def _tile(n, pref):
    t = min(n, pref)
    assert n % t == 0, (n, t)
    return t


def _rms(x, g, eps):
    return x * lax.rsqrt(jnp.mean(x * x, axis=-1, keepdims=True) + eps) * g


def _rope_table_kernel(pos_ref, invf_ref, o_ref):
    ang = pos_ref[...] * invf_ref[...]
    lane = lax.broadcasted_iota(jnp.int32, ang.shape, 1)
    o_ref[...] = jnp.where(lane < B_ROPE_DIM, jnp.cos(ang), jnp.sin(ang))


def _rope_table(pos_lanes, invf):
    t = pos_lanes.shape[0]
    tm = _tile(t, 1024)
    return pl.pallas_call(
        _rope_table_kernel,
        out_shape=jax.ShapeDtypeStruct((t, LANES), F32),
        grid=(t // tm,),
        in_specs=[pl.BlockSpec((tm, LANES), lambda i: (i, 0)),
                  pl.BlockSpec((1, LANES), lambda i: (0, 0))],
        out_specs=pl.BlockSpec((tm, LANES), lambda i: (i, 0)),
        compiler_params=_params("parallel"),
        name="rope_table",
    )(pos_lanes, invf)


def _norm_matmul_kernel(x_ref, g_ref, w_ref, o_ref, xn_ref):
    @pl.when(pl.program_id(1) == 0)
    def _():
        xn_ref[...] = _rms(x_ref[...], g_ref[...], NORM_EPS).astype(BF16)

    o_ref[...] = jnp.dot(xn_ref[...], w_ref[...],
                         preferred_element_type=F32).astype(o_ref.dtype)


def _norm_matmul(x, g, w, layer, out_dtype, name, tm=1024, tn=768):
    t, k = x.shape
    n = w.shape[2]
    tm = _tile(t, tm)
    tn = _tile(n, tn)
    return pl.pallas_call(
        _norm_matmul_kernel,
        out_shape=jax.ShapeDtypeStruct((t, n), out_dtype),
        grid=(t // tm, n // tn),
        in_specs=[pl.BlockSpec((tm, k), lambda i, j: (i, 0)),
                  pl.BlockSpec((1, k), lambda i, j: (0, 0)),
                  pl.BlockSpec((None, k, tn), lambda i, j: (layer, 0, j))],
        out_specs=pl.BlockSpec((tm, tn), lambda i, j: (i, j)),
        scratch_shapes=[pltpu.VMEM((tm, k), BF16)],
        compiler_params=_params("parallel", "arbitrary"),
        name=name,
    )(x, g, w)


def _matmul_res_kernel(*refs, n_lhs):
    lhs = refs[:n_lhs]
    ws = refs[n_lhs:2 * n_lhs]
    res_ref, o_ref = refs[2 * n_lhs], refs[2 * n_lhs + 1]
    acc = res_ref[...]
    for a, w in zip(lhs, ws):
        acc = acc + jnp.dot(a[...], w[...], preferred_element_type=F32)
    o_ref[...] = acc


def _matmul_res(lhs_list, w, layer, res, name, tm=512, tn=2048):
    t, n = res.shape
    tm = _tile(t, tm)
    tn = _tile(n, tn)
    n_lhs = len(lhs_list)
    kp = lhs_list[0].shape[1]
    assert all(a.shape[1] == kp for a in lhs_list) and n_lhs * kp == w.shape[1]
    in_specs = []
    for a in lhs_list:
        in_specs.append(pl.BlockSpec((tm, kp), lambda i, j: (i, 0)))
    for part in range(n_lhs):
        in_specs.append(pl.BlockSpec(
            (None, kp, tn), functools.partial(lambda i, j, part: (layer, part, j), part=part)))
    in_specs.append(pl.BlockSpec((tm, tn), lambda i, j: (i, j)))
    return pl.pallas_call(
        functools.partial(_matmul_res_kernel, n_lhs=n_lhs),
        out_shape=jax.ShapeDtypeStruct((t, n), F32),
        grid=(t // tm, n // tn),
        in_specs=in_specs,
        out_specs=pl.BlockSpec((tm, tn), lambda i, j: (i, j)),
        compiler_params=_params("parallel", "parallel"),
        name=name,
    )(*lhs_list, *([w] * n_lhs), res)


def _mlp_kernel(h_ref, g_ref, w1_ref, w2_ref, gf_ref, o_ref, xn_ref, *, final_norm):
    j = pl.program_id(1)

    @pl.when(j == 0)
    def _():
        h = h_ref[...]
        xn_ref[...] = _rms(h, g_ref[...], NORM_EPS).astype(BF16)
        o_ref[...] = h

    u = jnp.maximum(jnp.dot(xn_ref[...], w1_ref[...], preferred_element_type=F32), 0.0)
    u = (u * u).astype(BF16)
    o_ref[...] += jnp.dot(u, w2_ref[...], preferred_element_type=F32)

    if final_norm:
        @pl.when(j == pl.num_programs(1) - 1)
        def _():
            o_ref[...] = _rms(o_ref[...], gf_ref[...], NORM_EPS)


def _mlp(h, g, w1, w2, layer, gf, final_norm, name, tm=512, tf=1024):
    t, d = h.shape
    f = w1.shape[2]
    tm = _tile(t, tm)
    tf = _tile(f, tf)
    return pl.pallas_call(
        functools.partial(_mlp_kernel, final_norm=final_norm),
        out_shape=jax.ShapeDtypeStruct((t, d), F32),
        grid=(t // tm, f // tf),
        in_specs=[pl.BlockSpec((tm, d), lambda i, j: (i, 0)),
                  pl.BlockSpec((1, d), lambda i, j: (0, 0)),
                  pl.BlockSpec((None, d, tf), lambda i, j: (layer, 0, j)),
                  pl.BlockSpec((None, tf, d), lambda i, j: (layer, j, 0)),
                  pl.BlockSpec((1, d), lambda i, j: (0, 0))],
        out_specs=pl.BlockSpec((tm, d), lambda i, j: (i, 0)),
        scratch_shapes=[pltpu.VMEM((tm, d), BF16)],
        compiler_params=_params("parallel", "arbitrary"),
        name=name,
    )(h, g, w1, w2, gf)


def _mla_prep_kernel(cq_ref, ckv0_ref, ckv1_ref, kr_ref, cs_ref, gq_ref, gkv_ref, wq_ref,
                     wkv_ref, q_ref, kv_ref):
    cq = cq_ref[...].astype(F32)
    ckv = jnp.concatenate([ckv0_ref[...], ckv1_ref[...]], axis=1).astype(F32)
    cs = cs_ref[...]
    scale = B_QK_DIM ** -0.5 * LOG2E
    cqn = _rms(cq, gq_ref[...], NORM_EPS).astype(BF16)
    q = jnp.dot(cqn, wq_ref[...], preferred_element_type=F32)
    cs_scaled = cs * scale
    for h in range(B_HEADS):
        lo = h * 2 * LANES
        q_ref[:, lo:lo + LANES] = (q[:, lo:lo + LANES] * scale).astype(BF16)
        q_ref[:, lo + LANES:lo + 2 * LANES] = (
            q[:, lo + LANES:lo + 2 * LANES] * cs_scaled).astype(BF16)
    ckvn = _rms(ckv, gkv_ref[...], NORM_EPS).astype(BF16)
    kv = jnp.dot(ckvn, wkv_ref[...], preferred_element_type=F32)
    nk = B_HEADS * B_NOPE_DIM
    kv_ref[:, :nk] = kv[:, :nk].astype(BF16)
    kv_ref[:, nk + LANES:] = kv[:, nk:].astype(BF16)
    t = kr_ref[...].astype(F32) * cs
    kv_ref[:, nk:nk + LANES] = (t + pltpu.roll(t, B_ROPE_DIM, 1)).astype(BF16)


def _mla_prep(proj, cs, gq, gkv, wq, wkv, tm=512):
    t = proj.shape[0]
    tm = _tile(t, tm)
    assert C_CQ % B_Q_LORA == 0 and C_CKV % CKV_BLK == 0 and B_KV_LORA == 2 * CKV_BLK
    nq = B_HEADS * 2 * LANES
    nkv = B_HEADS * (B_NOPE_DIM + B_V_DIM) + LANES
    return pl.pallas_call(
        _mla_prep_kernel,
        out_shape=(jax.ShapeDtypeStruct((t, nq), BF16),
                   jax.ShapeDtypeStruct((t, nkv), BF16)),
        grid=(t // tm,),
        in_specs=[pl.BlockSpec((tm, B_Q_LORA), lambda i: (i, C_CQ // B_Q_LORA)),
                  pl.BlockSpec((tm, CKV_BLK), lambda i: (i, C_CKV // CKV_BLK)),
                  pl.BlockSpec((tm, CKV_BLK), lambda i: (i, C_CKV // CKV_BLK + 1)),
                  pl.BlockSpec((tm, LANES), lambda i: (i, C_KR // LANES)),
                  pl.BlockSpec((tm, LANES), lambda i: (i, 0)),
                  pl.BlockSpec((1, B_Q_LORA), lambda i: (0, 0)),
                  pl.BlockSpec((1, B_KV_LORA), lambda i: (0, 0)),
                  pl.BlockSpec(wq.shape, lambda i: (0, 0)),
                  pl.BlockSpec(wkv.shape, lambda i: (0, 0))],
        out_specs=(pl.BlockSpec((tm, nq), lambda i: (i, 0)),
                   pl.BlockSpec((tm, nkv), lambda i: (i, 0))),
        compiler_params=_params("parallel"),
        name="mla_prep",
    )(proj, proj, proj, proj, cs, gq, gkv, wq, wkv)


def _flash_sweep(q_at, k_at, v_at, finish, s_ref, p_ref, m_ref, al_ref, acc_ref, *,
                 nq, tq, rc):
    rows = m_ref.shape[0]
    tk = tq // 2
    nl = tk // LANES

    full = [(0, rows)]
    late = [(m * tq + tk, (m + 1) * tq) for m in range(rows // tq)]

    def qk(tile, kb, slot, spans):
        k = k_at(kb)
        for a, b in spans:
            s_ref[slot, a:b, :] = lax.dot_general(
                q_at(tile, a, b), k, (((1,), (1,)), ((), ())), preferred_element_type=F32)

    def pv(kb, slot, spans):
        v = v_at(kb)
        v1 = jnp.concatenate([v, jnp.ones_like(v)], axis=1)
        for a, b in spans:
            upd = jnp.dot(p_ref[slot, a:b, :], v1, preferred_element_type=F32)
            al = al_ref[slot, a:b, :]
            for j in range(2):
                ls = slice(j * LANES, (j + 1) * LANES)
                acc_ref[a:b, ls] = al * acc_ref[a:b, ls] + upd[:, ls]

    def softmax(s_slot, slot, spans, col0):
        def hidden(r0, j):
            return col0 is not None and col0 + j * LANES > r0 % tq + rc - 1

        def load(r0, j):
            x = s_ref[s_slot, r0:r0 + rc, j * LANES:(j + 1) * LANES]
            if col0 is not None and col0 + (j + 1) * LANES - 1 > r0 % tq:
                row = lax.broadcasted_iota(jnp.int32, x.shape, 0) + (r0 % tq)
                col = lax.broadcasted_iota(jnp.int32, x.shape, 1) + (col0 + j * LANES)
                x = jnp.where(col <= row, x, NEG_INF)
            return x

        chunks = [r0 for a, b in spans for r0 in range(a, b, rc)]
        for r0 in chunks:
            live = [j for j in range(nl) if not hidden(r0, j)]
            pm = load(r0, live[0])
            for j in live[1:]:
                pm = jnp.maximum(pm, load(r0, j))
            mb = jnp.broadcast_to(jnp.max(pm, axis=-1, keepdims=True), (rc, LANES))
            m_old = m_ref[r0:r0 + rc, :]
            m_new = jnp.maximum(m_old, mb)
            al_ref[slot, r0:r0 + rc, :] = jnp.exp2(m_old - m_new)
            m_ref[r0:r0 + rc, :] = m_new

        for r0 in chunks:
            m_new = m_ref[r0:r0 + rc, :]
            for j in range(nl):
                ls = slice(j * LANES, (j + 1) * LANES)
                if hidden(r0, j):
                    p_ref[slot, r0:r0 + rc, ls] = jnp.zeros((rc, LANES), BF16)
                else:
                    p_ref[slot, r0:r0 + rc, ls] = jnp.exp2(load(r0, j) - m_new).astype(BF16)

    qk(0, 0, 2, full)

    def tile(qi, carry):
        s_ref[0] = s_ref[2]
        m_ref[...] = jnp.full(m_ref.shape, NEG_INF, F32)
        acc_ref[...] = jnp.zeros(acc_ref.shape, F32)
        p_ref[1] = jnp.zeros(p_ref.shape[1:], BF16)
        al_ref[1] = jnp.ones(al_ref.shape[1:], F32)

        def pair(tp, c):
            e = 2 * tp
            qk(qi, e + 1, 1, full)
            softmax(0, 0, full, None)
            pv(jnp.maximum(e - 1, 0), 1, full)
            qk(qi, e + 2, 0, full)
            softmax(1, 1, full, None)
            pv(e, 0, full)
            return c

        lax.fori_loop(0, qi, pair, 0)
        e = 2 * qi
        qk(qi, e + 1, 1, late)
        softmax(0, 2, full, 0)
        pv(jnp.maximum(e - 1, 0), 1, full)
        qk(jnp.minimum(qi + 1, nq - 1), 0, 2, full)
        softmax(1, 3, late, tk)
        pv(e, 2, full)
        pv(e + 1, 3, late)
        finish(qi)
        return carry

    lax.fori_loop(0, nq, tile, 0)


def _flash_scratch(rows, tq, dv):
    tk = tq // 2
    return [pltpu.VMEM((3, rows, tk), F32),
            pltpu.VMEM((4, rows, tk), BF16),
            pltpu.VMEM((rows, LANES), F32),
            pltpu.VMEM((4, rows, LANES), F32),
            pltpu.VMEM((rows, 2 * dv), F32)]


def _diff_attn_kernel(q_ref, k_ref, v_ref, lam_ref, g_ref, o_ref, q_s, *scratch,
                      nq, tq, rc, lambda_init):
    rows = 2 * tq
    tk = tq // 2
    lane = lax.broadcasted_iota(jnp.int32, (tq, LANES), 1)
    zero = jnp.zeros((tq, LANES), BF16)
    for i in range(nq):
        q = q_ref[i * tq:(i + 1) * tq, :]
        q_s[i * rows:i * rows + tq, :] = jnp.where(lane < A_QK_DIM, q, zero)
        q_s[i * rows + tq:(i + 1) * rows, :] = jnp.where(lane < A_QK_DIM, zero, q)

    def q_at(tile, a, b):
        return q_s[pl.ds(pl.multiple_of(tile * rows + a, SUBLANES * 2), b - a), :]

    def k_at(kb):
        return k_ref[pl.ds(pl.multiple_of(kb * tk, tk), tk), :]

    def v_at(kb):
        return v_ref[pl.ds(pl.multiple_of(kb * tk, tk), tk), :]

    lv = lam_ref[...]
    lam = (jnp.exp(jnp.sum(lv[0:1] * lv[1:2], axis=-1, keepdims=True))
           - jnp.exp(jnp.sum(lv[2:3] * lv[3:4], axis=-1, keepdims=True))
           + lambda_init)
    acc_ref = scratch[-1]

    def finish(qi):
        acc = acc_ref[...]
        o = acc[:, :A_V_DIM] / acc[:, A_V_DIM:]
        d = o[:tq] - lam * o[tq:]
        o_ref[pl.ds(pl.multiple_of(qi * tq, tq), tq), :] = (
            _rms(d, g_ref[...], A_SUBLN_EPS) * (1.0 - lambda_init)).astype(o_ref.dtype)

    _flash_sweep(q_at, k_at, v_at, finish, *scratch, nq=nq, tq=tq, rc=rc)


def _diff_attn(proj, lamv, subln, batch, seq, lambda_init, tq=1024, rc=64):
    t = proj.shape[0]
    tq = _tile(seq, tq)
    nq = seq // tq
    qc, kc, vc = C_QA // LANES, C_KA // LANES, C_VA // LANES
    return pl.pallas_call(
        functools.partial(_diff_attn_kernel, nq=nq, tq=tq, rc=rc, lambda_init=lambda_init),
        out_shape=jax.ShapeDtypeStruct((t, A_HEADS * A_V_DIM), BF16),
        grid=(batch, A_HEADS),
        in_specs=[pl.BlockSpec((seq, LANES), lambda b, h: (b, qc + h)),
                  pl.BlockSpec((seq, LANES), lambda b, h: (b, kc + h)),
                  pl.BlockSpec((seq, LANES), lambda b, h: (b, vc + h)),
                  pl.BlockSpec(lamv.shape, lambda b, h: (0, 0)),
                  pl.BlockSpec((1, A_V_DIM), lambda b, h: (0, 0))],
        out_specs=pl.BlockSpec((seq, LANES), lambda b, h: (b, h)),
        scratch_shapes=[pltpu.VMEM((2 * seq, LANES), BF16)] + _flash_scratch(2 * tq, tq, A_V_DIM),
        compiler_params=_params("parallel", "parallel"),
        name="diff_attn",
    )(proj, proj, proj, lamv, subln)


def _mla_attn_kernel(q_ref, kn_ref, kr_ref, v_ref, o_ref, *scratch, nq, tq, rc):
    tk = tq // 2

    def q_at(tile, a, b):
        return q_ref[pl.ds(pl.multiple_of(tile * tq + a, SUBLANES * 2), b - a), :]

    def k_at(kb):
        sl = pl.ds(pl.multiple_of(kb * tk, tk), tk)
        return jnp.concatenate([kn_ref[sl, :], kr_ref[sl, :]], axis=1)

    def v_at(kb):
        return v_ref[pl.ds(pl.multiple_of(kb * tk, tk), tk), :]

    acc_ref = scratch[-1]

    def finish(qi):
        acc = acc_ref[...]
        o_ref[pl.ds(pl.multiple_of(qi * tq, tq), tq), :] = (
            acc[:, :B_V_DIM] / acc[:, B_V_DIM:]).astype(o_ref.dtype)

    _flash_sweep(q_at, k_at, v_at, finish, *scratch, nq=nq, tq=tq, rc=rc)


def _mla_attn(qb, kvb, batch, seq, tq=1024, rc=64):
    t = qb.shape[0]
    tq = _tile(seq, tq)
    nq = seq // tq
    return pl.pallas_call(
        functools.partial(_mla_attn_kernel, nq=nq, tq=tq, rc=rc),
        out_shape=jax.ShapeDtypeStruct((t, B_HEADS * B_V_DIM), BF16),
        grid=(batch, B_HEADS),
        in_specs=[pl.BlockSpec((seq, 2 * LANES), lambda b, h: (b, h)),
                  pl.BlockSpec((seq, LANES), lambda b, h: (b, h)),
                  pl.BlockSpec((seq, LANES), lambda b, h: (b, B_HEADS)),
                  pl.BlockSpec((seq, LANES), lambda b, h: (b, B_HEADS + 1 + h))],
        out_specs=pl.BlockSpec((seq, LANES), lambda b, h: (b, h)),
        scratch_shapes=_flash_scratch(tq, tq, B_V_DIM),
        compiler_params=_params("parallel", "parallel"),
        name="mla_attn",
    )(qb, kvb, kvb, kvb)


def _rglru_kernel(y_ref, x_ref, cw_ref, cb_ref, wg_ref, bg_ref, lam_ref, o_ref,
                  halo_ref, h_ref, xs_ref, hs_ref, *, ts, bw):
    seg = ts // SUBLANES
    pitch = xs_ref.shape[1] // SUBLANES
    nlc = bw // LANES

    def put(ref, r0, v):
        for c in range(nlc):
            ref[c, r0:r0 + v.shape[0], :] = v[:, c * LANES:(c + 1) * LANES]

    def take(ref, start, stride):
        return jnp.concatenate(
            [ref[c, pl.ds(start, SUBLANES, stride=stride), :] for c in range(nlc)], axis=1)

    @pl.when(pl.program_id(2) == 0)
    def _():
        halo_ref[...] = jnp.zeros_like(halo_ref)
        h_ref[...] = jnp.zeros_like(h_ref)

    x = x_ref[...].astype(F32)
    for s in range(SUBLANES):
        put(xs_ref, s * pitch, x[s * seg:(s + 1) * seg])
    xp = [take(xs_ref, g, pitch) for g in range(seg)]
    sub = lax.broadcasted_iota(jnp.int32, (SUBLANES, bw), 0)
    halo = halo_ref[...]
    before = {}
    for k in range(1, CONV_WIDTH):
        prev_tile = halo[SUBLANES - k:SUBLANES - k + 1]
        before[-k] = jnp.where(sub == 0, prev_tile, pltpu.roll(xp[seg - k], 1, 0))
    halo_ref[...] = x[ts - SUBLANES:]
    cw = cw_ref[...]
    cb = cb_ref[...]
    xc = []
    for g in range(seg):
        acc = cb + cw[CONV_WIDTH - 1:CONV_WIDTH] * xp[g]
        for k in range(1, CONV_WIDTH):
            src = xp[g - k] if g - k >= 0 else before[g - k]
            acc = acc + cw[CONV_WIDTH - 1 - k:CONV_WIDTH - k] * src
        xc.append(acc)
    xc = jnp.concatenate(xc, axis=0)

    gates = jnp.dot(xc.astype(BF16), wg_ref[...], preferred_element_type=F32) + bg_ref[...]
    r = jax.nn.sigmoid(gates[:, :bw])
    i = jax.nn.sigmoid(gates[:, bw:])
    decay = -LRU_C * jax.nn.softplus(-lam_ref[...])
    log_a = r * decay
    a = jnp.exp2(r * (decay * LOG2E))
    b = jnp.sqrt(-jnp.tanh(log_a) * (a * a + 1.0)) * (i * xc)

    hl, ap = [], []
    h = jnp.zeros((SUBLANES, bw), F32)
    prod = jnp.ones((SUBLANES, bw), F32)
    for g in range(seg):
        ag = a[g * SUBLANES:(g + 1) * SUBLANES]
        h = ag * h + b[g * SUBLANES:(g + 1) * SUBLANES]
        prod = ag * prod
        hl.append(h)
        ap.append(prod)
    c = h_ref[...]
    carry = []
    for s in range(SUBLANES):
        carry.append(c)
        c = prod[s:s + 1] * c + h[s:s + 1]
    h_ref[...] = c
    carry = jnp.concatenate(carry, axis=0)
    for g in range(seg):
        put(hs_ref, g * SUBLANES, hl[g] + ap[g] * carry)
    rows = []
    for n in range(ts // SUBLANES):
        start = ((n * SUBLANES) % seg) * SUBLANES + (n * SUBLANES) // seg
        rows.append(take(hs_ref, start, SUBLANES))
    hs = jnp.concatenate(rows, axis=0)
    y = jax.nn.gelu(y_ref[...].astype(F32), approximate=True)
    o_ref[...] = (y * hs).astype(o_ref.dtype)


def _rglru(proj, conv_w, conv_b, wg, bg, lam, batch, seq, ts=1024):
    t = proj.shape[0]
    width = proj.shape[1] // 2
    bw = width // LRU_BLOCKS
    ts = _tile(seq, ts)
    ns = seq // ts
    return pl.pallas_call(
        functools.partial(_rglru_kernel, ts=ts, bw=bw),
        out_shape=jax.ShapeDtypeStruct((t, width), BF16),
        grid=(batch, LRU_BLOCKS, ns),
        in_specs=[pl.BlockSpec((ts, bw), lambda b, n, s: (b * ns + s, n)),
                  pl.BlockSpec((ts, bw), lambda b, n, s: (b * ns + s, LRU_BLOCKS + n)),
                  pl.BlockSpec((CONV_WIDTH, bw), lambda b, n, s: (0, n)),
                  pl.BlockSpec((1, bw), lambda b, n, s: (0, n)),
                  pl.BlockSpec((None, bw, 2 * bw), lambda b, n, s: (n, 0, 0)),
                  pl.BlockSpec((None, 1, 2 * bw), lambda b, n, s: (n, 0, 0)),
                  pl.BlockSpec((1, bw), lambda b, n, s: (0, n))],
        out_specs=pl.BlockSpec((ts, bw), lambda b, n, s: (b * ns + s, n)),
        scratch_shapes=[pltpu.VMEM((SUBLANES, bw), F32), pltpu.VMEM((1, bw), F32),
                        pltpu.VMEM((bw // LANES, ts + SUBLANES * SUBLANES, LANES), F32),
                        pltpu.VMEM((bw // LANES, ts, LANES), F32)],
        compiler_params=_params("parallel", "parallel", "arbitrary"),
        name="rglru",
    )(proj, proj, conv_w, conv_b, wg, bg, lam)


def _rot_half_cols(w):
    half = w.shape[-1] // 2
    return jnp.concatenate([-w[..., half:], w[..., :half]], axis=-1)


def _prep_attn_w_in(w):
    k, n = w.shape
    qa_cols = A_HEADS * 2 * A_QK_DIM
    col_scale = jnp.where(jnp.arange(n) < qa_cols, A_QK_DIM ** -0.5 * LOG2E, 1.0).astype(w.dtype)
    kr = w[:, C_KR:]
    return jnp.concatenate(
        [(w * col_scale).astype(BF16), _rot_half_cols(kr).astype(BF16),
         jnp.zeros((k, C_PAD - C_END), BF16)], axis=1)


def _prep_w_uq(w):
    k = w.shape[0]
    w = w.reshape(k, B_HEADS, B_QK_DIM)
    rope = w[:, :, B_NOPE_DIM:]
    return jnp.concatenate([w[:, :, :B_NOPE_DIM], rope, _rot_half_cols(rope)],
                           axis=-1).reshape(k, B_HEADS * 2 * LANES).astype(BF16)


def _prep_w_ukv(w):
    k = w.shape[0]
    w = w.reshape(k, B_HEADS, B_NOPE_DIM + B_V_DIM)
    return jnp.concatenate([w[:, :, :B_NOPE_DIM].reshape(k, -1),
                            w[:, :, B_NOPE_DIM:].reshape(k, -1)], axis=1).astype(BF16)


def kernel(x, positions, norm_mix, norm_mlp, norm_final, attn_w_in, attn_lambda_q1,
           attn_lambda_k1, attn_lambda_q2, attn_lambda_k2, attn_subln, attn_q_norm,
           attn_kv_norm, attn_w_uq, attn_w_ukv, attn_w_out, rec_w_in, rec_conv_w,
           rec_conv_b, rec_w_a, rec_b_a, rec_w_x, rec_b_x, rec_lambda, rec_w_out,
           mlp_w1, mlp_w2):
    batch, seq, d = x.shape
    t = batch * seq
    h0 = x.reshape(t, d)
    row = lambda v: v.reshape(1, -1)

    half = B_ROPE_DIM // 2
    inv_freq = 1.0 / (ROPE_THETA ** (jnp.arange(0, B_ROPE_DIM, 2, dtype=F32) / B_ROPE_DIM))
    invf = jnp.tile(inv_freq, LANES // half).reshape(1, LANES)
    pos_lanes = jnp.broadcast_to(positions.astype(F32).reshape(t, 1), (t, LANES))
    cs = _rope_table(pos_lanes, invf)

    proj = _norm_matmul(h0, row(norm_mix[0]), _prep_attn_w_in(attn_w_in[0])[None], 0, BF16,
                        "attn_in_proj", tm=512, tn=C_PAD)
    qb, kvb = _mla_prep(proj, cs, row(attn_q_norm[0]), row(attn_kv_norm[0]),
                        _prep_w_uq(attn_w_uq[0]), _prep_w_ukv(attn_w_ukv[0]))
    lambda_init = 0.8 - 0.6 * math.exp(-0.3 * 0)
    lamv = jnp.stack([attn_lambda_q1[0], attn_lambda_k1[0],
                      attn_lambda_q2[0], attn_lambda_k2[0]]).astype(F32)
    oa = _diff_attn(proj, lamv, row(attn_subln[0]), batch, seq, lambda_init)
    ob = _mla_attn(qb, kvb, batch, seq)
    h1 = _matmul_res([oa, ob], attn_w_out.astype(BF16), 0, h0, "attn_out_proj")
    w1 = mlp_w1.astype(BF16)
    w2 = mlp_w2.astype(BF16)
    h2 = _mlp(h1, row(norm_mlp[0]), w1, w2, 0, row(norm_final), False, "mlp0")

    proj2 = _norm_matmul(h2, row(norm_mix[1]), rec_w_in.astype(BF16), 0, BF16,
                         "rec_in_proj", tn=2048)
    wg = jnp.concatenate([rec_w_a[0], rec_w_x[0]], axis=-1).astype(BF16)
    bg = jnp.concatenate([rec_b_a[0], rec_b_x[0]], axis=-1)[:, None, :]
    g = _rglru(proj2, rec_conv_w[0], row(rec_conv_b[0]), wg, bg, row(rec_lambda[0]),
               batch, seq)
    h3 = _matmul_res([g], rec_w_out.astype(BF16), 0, h2, "rec_out_proj")
    out = _mlp(h3, row(norm_mlp[1]), w1, w2, 1, row(norm_final), True, "mlp1")
    return out.reshape(batch, seq, d)
```

```python
import functools
import math

import jax
import jax.numpy as jnp
from jax import lax
from jax.experimental import pallas as pl
from jax.experimental.pallas import tpu as pltpu

F32 = jnp.float32
BF16 = jnp.bfloat16

NORM_EPS = 1e-6
NEG_INF = -1e30
A_HEADS = 8
A_QK_DIM = 64
A_V_DIM = 128
A_SUBLN_EPS = 1e-5
B_HEADS = 8
B_Q_LORA = 768
B_KV_LORA = 512
B_NOPE_DIM = 128
B_ROPE_DIM = 64
B_V_DIM = 128
B_QK_DIM = B_NOPE_DIM + B_ROPE_DIM
ROPE_THETA = 10000.0
LRU_BLOCKS = 8
CONV_WIDTH = 4
LRU_C = 8.0
LOG2E = math.log2(math.e)

LANES = 128
SUBLANES = 8
VMEM_LIMIT = 56 * 1024 * 1024

MXU_N = 256
C_QA = 0
C_KA = C_QA + A_HEADS * 2 * A_QK_DIM
C_VA = C_KA + A_HEADS * 2 * A_QK_DIM
C_CQ = C_VA + A_HEADS * A_V_DIM
C_CKV = C_CQ + B_Q_LORA
C_KR = C_CKV + B_KV_LORA
C_END = C_KR + 2 * B_ROPE_DIM
C_PAD = -(-C_END // (3 * MXU_N)) * (3 * MXU_N)
CKV_BLK = 256


def _params(*sem):
    return pltpu.CompilerParams(dimension_semantics=sem, vmem_limit_bytes=VMEM_LIMIT)


def _tile(n, pref):
    t = min(n, pref)
    assert n % t == 0, (n, t)
    return t


def _rms(x, g, eps):
    return x * lax.rsqrt(jnp.mean(x * x, axis=-1, keepdims=True) + eps) * g


def _rope_table_kernel(pos_ref, invf_ref, o_ref):
    ang = pos_ref[...] * invf_ref[...]
    lane = lax.broadcasted_iota(jnp.int32, ang.shape, 1)
    o_ref[...] = jnp.where(lane < B_ROPE_DIM, jnp.cos(ang), jnp.sin(ang))


def _rope_table(pos_lanes, invf):
    t = pos_lanes.shape[0]
    tm = _tile(t, 1024)
    return pl.pallas_call(
        _rope_table_kernel,
        out_shape=jax.ShapeDtypeStruct((t, LANES), F32),
        grid=(t // tm,),
        in_specs=[pl.BlockSpec((tm, LANES), lambda i: (i, 0)),
                  pl.BlockSpec((1, LANES), lambda i: (0, 0))],
        out_specs=pl.BlockSpec((tm, LANES), lambda i: (i, 0)),
        compiler_params=_params("parallel"),
        name="rope_table",
    )(pos_lanes, invf)


def _norm_matmul_kernel(x_ref, g_ref, w_ref, o_ref, xn_ref):
    @pl.when(pl.program_id(1) == 0)
    def _():
        xn_ref[...] = _rms(x_ref[...], g_ref[...], NORM_EPS).astype(BF16)

    o_ref[...] = jnp.dot(xn_ref[...], w_ref[...],
                         preferred_element_type=F32).astype(o_ref.dtype)


def _norm_matmul(x, g, w, layer, out_dtype, name, tm=1024, tn=768):
    t, k = x.shape
    n = w.shape[2]
    tm = _tile(t, tm)
    tn = _tile(n, tn)
    return pl.pallas_call(
        _norm_matmul_kernel,
        out_shape=jax.ShapeDtypeStruct((t, n), out_dtype),
        grid=(t // tm, n // tn),
        in_specs=[pl.BlockSpec((tm, k), lambda i, j: (i, 0)),
                  pl.BlockSpec((1, k), lambda i, j: (0, 0)),
                  pl.BlockSpec((None, k, tn), lambda i, j: (layer, 0, j))],
        out_specs=pl.BlockSpec((tm, tn), lambda i, j: (i, j)),
        scratch_shapes=[pltpu.VMEM((tm, k), BF16)],
        compiler_params=_params("parallel", "arbitrary"),
        name=name,
    )(x, g, w)


def _matmul_res_kernel(*refs, n_lhs):
    lhs = refs[:n_lhs]
    ws = refs[n_lhs:2 * n_lhs]
    res_ref, o_ref = refs[2 * n_lhs], refs[2 * n_lhs + 1]
    acc = res_ref[...]
    for a, w in zip(lhs, ws):
        acc = acc + jnp.dot(a[...], w[...], preferred_element_type=F32)
    o_ref[...] = acc


def _matmul_res(lhs_list, w, layer, res, name, tm=512, tn=2048):
    t, n = res.shape
    tm = _tile(t, tm)
    tn = _tile(n, tn)
    n_lhs = len(lhs_list)
    kp = lhs_list[0].shape[1]
    assert all(a.shape[1] == kp for a in lhs_list) and n_lhs * kp == w.shape[1]
    in_specs = []
    for a in lhs_list:
        in_specs.append(pl.BlockSpec((tm, kp), lambda i, j: (i, 0)))
    for part in range(n_lhs):
        in_specs.append(pl.BlockSpec(
            (None, kp, tn), functools.partial(lambda i, j, part: (layer, part, j), part=part)))
    in_specs.append(pl.BlockSpec((tm, tn), lambda i, j: (i, j)))
    return pl.pallas_call(
        functools.partial(_matmul_res_kernel, n_lhs=n_lhs),
        out_shape=jax.ShapeDtypeStruct((t, n), F32),
        grid=(t // tm, n // tn),
        in_specs=in_specs,
        out_specs=pl.BlockSpec((tm, tn), lambda i, j: (i, j)),
        compiler_params=_params("parallel", "parallel"),
        name=name,
    )(*lhs_list, *([w] * n_lhs), res)


def _mlp_kernel(*refs, n_cast, final_norm):
    h_ref, g_ref, w1_ref, w2_ref, gf_ref = refs[:5]
    cast_in = refs[5:5 + n_cast]
    o_ref = refs[5 + n_cast]
    cast_out = refs[6 + n_cast:6 + 2 * n_cast]
    xn_ref = refs[6 + 2 * n_cast]
    j = pl.program_id(1)

    @pl.when(j == 0)
    def _():
        h = h_ref[...]
        xn_ref[...] = _rms(h, g_ref[...], NORM_EPS).astype(BF16)
        o_ref[...] = h

    for src, dst in zip(cast_in, cast_out):
        dst[...] = src[...].astype(BF16)

    u = jnp.maximum(jnp.dot(xn_ref[...], w1_ref[...], preferred_element_type=F32), 0.0)
    u = (u * u).astype(BF16)
    o_ref[...] += jnp.dot(u, w2_ref[...], preferred_element_type=F32)

    if final_norm:
        @pl.when(j == pl.num_programs(1) - 1)
        def _():
            o_ref[...] = _rms(o_ref[...], gf_ref[...], NORM_EPS)


def _mlp(h, g, w1, w2, layer, gf, final_norm, name, cast=(), tm=512, tf=1024):
    t, d = h.shape
    f = w1.shape[2]
    tm = _tile(t, tm)
    tf = _tile(f, tf)
    nt, nj = t // tm, f // tf
    steps = nt * nj
    bf16_rows = 2 * SUBLANES
    in_specs = [pl.BlockSpec((tm, d), lambda i, j: (i, 0)),
                pl.BlockSpec((1, d), lambda i, j: (0, 0)),
                pl.BlockSpec((None, d, tf), lambda i, j: (layer, 0, j)),
                pl.BlockSpec((None, tf, d), lambda i, j: (layer, j, 0)),
                pl.BlockSpec((1, d), lambda i, j: (0, 0))]
    out_shape = [jax.ShapeDtypeStruct((t, d), F32)]
    out_specs = [pl.BlockSpec((tm, d), lambda i, j: (i, 0))]
    for w, lyr in cast:
        _, r, c = w.shape
        rows = max(bf16_rows, r // steps)
        nblk = r // rows
        assert r % rows == 0 and steps % nblk == 0, (w.shape, steps)
        per = steps // nblk
        in_specs.append(pl.BlockSpec(
            (None, rows, c), functools.partial(lambda i, j, lyr, per: (lyr, (i * nj + j) // per, 0),
                                               lyr=lyr, per=per)))
        out_shape.append(jax.ShapeDtypeStruct((r, c), BF16))
        out_specs.append(pl.BlockSpec(
            (rows, c), functools.partial(lambda i, j, per: ((i * nj + j) // per, 0), per=per)))
    res = pl.pallas_call(
        functools.partial(_mlp_kernel, n_cast=len(cast), final_norm=final_norm),
        out_shape=tuple(out_shape),
        grid=(nt, nj),
        in_specs=in_specs,
        out_specs=tuple(out_specs),
        scratch_shapes=[pltpu.VMEM((tm, d), BF16)],
        compiler_params=_params("arbitrary", "arbitrary"),
        name=name,
    )(h, g, w1, w2, gf, *[w for w, _ in cast])
    return res if cast else res[0]


def _mla_prep_kernel(cq_ref, ckv0_ref, ckv1_ref, kr_ref, cs_ref, gq_ref, gkv_ref, wq_ref,
                     wkv_ref, q_ref, kv_ref):
    cq = cq_ref[...].astype(F32)
    ckv = jnp.concatenate([ckv0_ref[...], ckv1_ref[...]], axis=1).astype(F32)
    cs = cs_ref[...]
    scale = B_QK_DIM ** -0.5 * LOG2E
    cqn = _rms(cq, gq_ref[...], NORM_EPS).astype(BF16)
    q = jnp.dot(cqn, wq_ref[...], preferred_element_type=F32)
    cs_scaled = cs * scale
    for h in range(B_HEADS):
        lo = h * 2 * LANES
        q_ref[:, lo:lo + LANES] = (q[:, lo:lo + LANES] * scale).astype(BF16)
        q_ref[:, lo + LANES:lo + 2 * LANES] = (
            q[:, lo + LANES:lo + 2 * LANES] * cs_scaled).astype(BF16)
    ckvn = _rms(ckv, gkv_ref[...], NORM_EPS).astype(BF16)
    kv = jnp.dot(ckvn, wkv_ref[...], preferred_element_type=F32)
    nk = B_HEADS * B_NOPE_DIM
    kv_ref[:, :nk] = kv[:, :nk].astype(BF16)
    kv_ref[:, nk + LANES:] = kv[:, nk:].astype(BF16)
    t = kr_ref[...].astype(F32) * cs
    kv_ref[:, nk:nk + LANES] = (t + pltpu.roll(t, B_ROPE_DIM, 1)).astype(BF16)


def _mla_prep(proj, cs, gq, gkv, wq, wkv, tm=512):
    t = proj.shape[0]
    tm = _tile(t, tm)
    assert C_CQ % B_Q_LORA == 0 and C_CKV % CKV_BLK == 0 and B_KV_LORA == 2 * CKV_BLK
    nq = B_HEADS * 2 * LANES
    nkv = B_HEADS * (B_NOPE_DIM + B_V_DIM) + LANES
    return pl.pallas_call(
        _mla_prep_kernel,
        out_shape=(jax.ShapeDtypeStruct((t, nq), BF16),
                   jax.ShapeDtypeStruct((t, nkv), BF16)),
        grid=(t // tm,),
        in_specs=[pl.BlockSpec((tm, B_Q_LORA), lambda i: (i, C_CQ // B_Q_LORA)),
                  pl.BlockSpec((tm, CKV_BLK), lambda i: (i, C_CKV // CKV_BLK)),
                  pl.BlockSpec((tm, CKV_BLK), lambda i: (i, C_CKV // CKV_BLK + 1)),
                  pl.BlockSpec((tm, LANES), lambda i: (i, C_KR // LANES)),
                  pl.BlockSpec((tm, LANES), lambda i: (i, 0)),
                  pl.BlockSpec((1, B_Q_LORA), lambda i: (0, 0)),
                  pl.BlockSpec((1, B_KV_LORA), lambda i: (0, 0)),
                  pl.BlockSpec(wq.shape, lambda i: (0, 0)),
                  pl.BlockSpec(wkv.shape, lambda i: (0, 0))],
        out_specs=(pl.BlockSpec((tm, nq), lambda i: (i, 0)),
                   pl.BlockSpec((tm, nkv), lambda i: (i, 0))),
        compiler_params=_params("parallel"),
        name="mla_prep",
    )(proj, proj, proj, proj, cs, gq, gkv, wq, wkv)


def _flash_sweep(q_at, k_at, v_at, finish, s_ref, p_ref, m_ref, al_ref, acc_ref, *,
                 nq, tq, rc):
    rows = m_ref.shape[0]
    tk = tq // 2
    nl = tk // LANES

    full = [(0, rows)]
    late = [(m * tq + tk, (m + 1) * tq) for m in range(rows // tq)]

    def qk(tile, kb, slot, spans):
        k = k_at(kb)
        for a, b in spans:
            s_ref[slot, a:b, :] = lax.dot_general(
                q_at(tile, a, b), k, (((1,), (1,)), ((), ())), preferred_element_type=F32)

    def pv(kb, slot, spans):
        v = v_at(kb)
        v1 = jnp.concatenate([v, jnp.ones_like(v)], axis=1)
        for a, b in spans:
            upd = jnp.dot(p_ref[slot, a:b, :], v1, preferred_element_type=F32)
            al = al_ref[slot, a:b, :]
            for j in range(2):
                ls = slice(j * LANES, (j + 1) * LANES)
                acc_ref[a:b, ls] = al * acc_ref[a:b, ls] + upd[:, ls]

    def softmax(s_slot, slot, spans, col0):
        def hidden(r0, j):
            return col0 is not None and col0 + j * LANES > r0 % tq + rc - 1

        def load(r0, j):
            x = s_ref[s_slot, r0:r0 + rc, j * LANES:(j + 1) * LANES]
            if col0 is not None and col0 + (j + 1) * LANES - 1 > r0 % tq:
                row = lax.broadcasted_iota(jnp.int32, x.shape, 0) + (r0 % tq)
                col = lax.broadcasted_iota(jnp.int32, x.shape, 1) + (col0 + j * LANES)
                x = jnp.where(col <= row, x, NEG_INF)
            return x

        chunks = [r0 for a, b in spans for r0 in range(a, b, rc)]
        for r0 in chunks:
            live = [j for j in range(nl) if not hidden(r0, j)]
            pm = load(r0, live[0])
            for j in live[1:]:
                pm = jnp.maximum(pm, load(r0, j))
            mb = jnp.broadcast_to(jnp.max(pm, axis=-1, keepdims=True), (rc, LANES))
            m_old = m_ref[r0:r0 + rc, :]
            m_new = jnp.maximum(m_old, mb)
            al_ref[slot, r0:r0 + rc, :] = jnp.exp2(m_old - m_new)
            m_ref[r0:r0 + rc, :] = m_new

        for r0 in chunks:
            m_new = m_ref[r0:r0 + rc, :]
            for j in range(nl):
                ls = slice(j * LANES, (j + 1) * LANES)
                if hidden(r0, j):
                    p_ref[slot, r0:r0 + rc, ls] = jnp.zeros((rc, LANES), BF16)
                else:
                    p_ref[slot, r0:r0 + rc, ls] = jnp.exp2(load(r0, j) - m_new).astype(BF16)

    qk(0, 0, 2, full)

    def tile(qi, carry):
        s_ref[0] = s_ref[2]
        m_ref[...] = jnp.full(m_ref.shape, NEG_INF, F32)
        acc_ref[...] = jnp.zeros(acc_ref.shape, F32)
        p_ref[1] = jnp.zeros(p_ref.shape[1:], BF16)
        al_ref[1] = jnp.ones(al_ref.shape[1:], F32)

        def pair(tp, c):
            e = 2 * tp
            qk(qi, e + 1, 1, full)
            softmax(0, 0, full, None)
            pv(jnp.maximum(e - 1, 0), 1, full)
            qk(qi, e + 2, 0, full)
            softmax(1, 1, full, None)
            pv(e, 0, full)
            return c

        lax.fori_loop(0, qi, pair, 0)
        e = 2 * qi
        qk(qi, e + 1, 1, late)
        softmax(0, 2, full, 0)
        pv(jnp.maximum(e - 1, 0), 1, full)
        qk(jnp.minimum(qi + 1, nq - 1), 0, 2, full)
        softmax(1, 3, late, tk)
        pv(e, 2, full)
        pv(e + 1, 3, late)
        finish(qi)
        return carry

    lax.fori_loop(0, nq, tile, 0)


def _flash_scratch(rows, tq, dv):
    tk = tq // 2
    return [pltpu.VMEM((3, rows, tk), F32),
            pltpu.VMEM((4, rows, tk), BF16),
            pltpu.VMEM((rows, LANES), F32),
            pltpu.VMEM((4, rows, LANES), F32),
            pltpu.VMEM((rows, 2 * dv), F32)]


def _diff_attn_kernel(q_ref, k_ref, v_ref, lam_ref, g_ref, o_ref, q_s, *scratch,
                      nq, tq, rc, lambda_init):
    rows = 2 * tq
    tk = tq // 2
    lane = lax.broadcasted_iota(jnp.int32, (tq, LANES), 1)
    zero = jnp.zeros((tq, LANES), BF16)
    for i in range(nq):
        q = q_ref[i * tq:(i + 1) * tq, :]
        q_s[i * rows:i * rows + tq, :] = jnp.where(lane < A_QK_DIM, q, zero)
        q_s[i * rows + tq:(i + 1) * rows, :] = jnp.where(lane < A_QK_DIM, zero, q)

    def q_at(tile, a, b):
        return q_s[pl.ds(pl.multiple_of(tile * rows + a, SUBLANES * 2), b - a), :]

    def k_at(kb):
        return k_ref[pl.ds(pl.multiple_of(kb * tk, tk), tk), :]

    def v_at(kb):
        return v_ref[pl.ds(pl.multiple_of(kb * tk, tk), tk), :]

    lv = lam_ref[...]
    lam = (jnp.exp(jnp.sum(lv[0:1] * lv[1:2], axis=-1, keepdims=True))
           - jnp.exp(jnp.sum(lv[2:3] * lv[3:4], axis=-1, keepdims=True))
           + lambda_init)
    acc_ref = scratch[-1]

    def finish(qi):
        acc = acc_ref[...]
        o = acc[:, :A_V_DIM] / acc[:, A_V_DIM:]
        d = o[:tq] - lam * o[tq:]
        o_ref[pl.ds(pl.multiple_of(qi * tq, tq), tq), :] = (
            _rms(d, g_ref[...], A_SUBLN_EPS) * (1.0 - lambda_init)).astype(o_ref.dtype)

    _flash_sweep(q_at, k_at, v_at, finish, *scratch, nq=nq, tq=tq, rc=rc)


def _diff_attn(proj, lamv, subln, batch, seq, lambda_init, tq=1024, rc=64):
    t = proj.shape[0]
    tq = _tile(seq, tq)
    nq = seq // tq
    qc, kc, vc = C_QA // LANES, C_KA // LANES, C_VA // LANES
    return pl.pallas_call(
        functools.partial(_diff_attn_kernel, nq=nq, tq=tq, rc=rc, lambda_init=lambda_init),
        out_shape=jax.ShapeDtypeStruct((t, A_HEADS * A_V_DIM), BF16),
        grid=(batch, A_HEADS),
        in_specs=[pl.BlockSpec((seq, LANES), lambda b, h: (b, qc + h)),
                  pl.BlockSpec((seq, LANES), lambda b, h: (b, kc + h)),
                  pl.BlockSpec((seq, LANES), lambda b, h: (b, vc + h)),
                  pl.BlockSpec(lamv.shape, lambda b, h: (0, 0)),
                  pl.BlockSpec((1, A_V_DIM), lambda b, h: (0, 0))],
        out_specs=pl.BlockSpec((seq, LANES), lambda b, h: (b, h)),
        scratch_shapes=[pltpu.VMEM((2 * seq, LANES), BF16)] + _flash_scratch(2 * tq, tq, A_V_DIM),
        compiler_params=_params("parallel", "parallel"),
        name="diff_attn",
    )(proj, proj, proj, lamv, subln)


def _mla_attn_kernel(q_ref, kn_ref, kr_ref, v_ref, o_ref, *scratch, nq, tq, rc):
    tk = tq // 2

    def q_at(tile, a, b):
        return q_ref[pl.ds(pl.multiple_of(tile * tq + a, SUBLANES * 2), b - a), :]

    def k_at(kb):
        sl = pl.ds(pl.multiple_of(kb * tk, tk), tk)
        return jnp.concatenate([kn_ref[sl, :], kr_ref[sl, :]], axis=1)

    def v_at(kb):
        return v_ref[pl.ds(pl.multiple_of(kb * tk, tk), tk), :]

    acc_ref = scratch[-1]

    def finish(qi):
        acc = acc_ref[...]
        o_ref[pl.ds(pl.multiple_of(qi * tq, tq), tq), :] = (
            acc[:, :B_V_DIM] / acc[:, B_V_DIM:]).astype(o_ref.dtype)

    _flash_sweep(q_at, k_at, v_at, finish, *scratch, nq=nq, tq=tq, rc=rc)


def _mla_attn(qb, kvb, batch, seq, tq=1024, rc=64):
    t = qb.shape[0]
    tq = _tile(seq, tq)
    nq = seq // tq
    return pl.pallas_call(
        functools.partial(_mla_attn_kernel, nq=nq, tq=tq, rc=rc),
        out_shape=jax.ShapeDtypeStruct((t, B_HEADS * B_V_DIM), BF16),
        grid=(batch, B_HEADS),
        in_specs=[pl.BlockSpec((seq, 2 * LANES), lambda b, h: (b, h)),
                  pl.BlockSpec((seq, LANES), lambda b, h: (b, h)),
                  pl.BlockSpec((seq, LANES), lambda b, h: (b, B_HEADS)),
                  pl.BlockSpec((seq, LANES), lambda b, h: (b, B_HEADS + 1 + h))],
        out_specs=pl.BlockSpec((seq, LANES), lambda b, h: (b, h)),
        scratch_shapes=_flash_scratch(tq, tq, B_V_DIM),
        compiler_params=_params("parallel", "parallel"),
        name="mla_attn",
    )(qb, kvb, kvb, kvb)


def _rglru_kernel(y_ref, x_ref, cw_ref, cb_ref, wg_ref, bg_ref, lam_ref, o_ref,
                  halo_ref, h_ref, xs_ref, hs_ref, *, ts, bw):
    seg = ts // SUBLANES
    pitch = xs_ref.shape[1] // SUBLANES
    nlc = bw // LANES

    def put(ref, r0, v):
        for c in range(nlc):
            ref[c, r0:r0 + v.shape[0], :] = v[:, c * LANES:(c + 1) * LANES]

    def take(ref, start, stride):
        return jnp.concatenate(
            [ref[c, pl.ds(start, SUBLANES, stride=stride), :] for c in range(nlc)], axis=1)

    @pl.when(pl.program_id(2) == 0)
    def _():
        halo_ref[...] = jnp.zeros_like(halo_ref)
        h_ref[...] = jnp.zeros_like(h_ref)

    x = x_ref[...].astype(F32)
    for s in range(SUBLANES):
        put(xs_ref, s * pitch, x[s * seg:(s + 1) * seg])
    xp = [take(xs_ref, g, pitch) for g in range(seg)]
    sub = lax.broadcasted_iota(jnp.int32, (SUBLANES, bw), 0)
    halo = halo_ref[...]
    before = {}
    for k in range(1, CONV_WIDTH):
        prev_tile = halo[SUBLANES - k:SUBLANES - k + 1]
        before[-k] = jnp.where(sub == 0, prev_tile, pltpu.roll(xp[seg - k], 1, 0))
    halo_ref[...] = x[ts - SUBLANES:]
    cw = cw_ref[...]
    cb = cb_ref[...]
    xc = []
    for g in range(seg):
        acc = cb + cw[CONV_WIDTH - 1:CONV_WIDTH] * xp[g]
        for k in range(1, CONV_WIDTH):
            src = xp[g - k] if g - k >= 0 else before[g - k]
            acc = acc + cw[CONV_WIDTH - 1 - k:CONV_WIDTH - k] * src
        xc.append(acc)
    xc = jnp.concatenate(xc, axis=0)

    gates = jnp.dot(xc.astype(BF16), wg_ref[...], preferred_element_type=F32) + bg_ref[...]
    r = jax.nn.sigmoid(gates[:, :bw])
    i = jax.nn.sigmoid(gates[:, bw:])
    decay = -LRU_C * jax.nn.softplus(-lam_ref[...])
    log_a = r * decay
    a = jnp.exp2(r * (decay * LOG2E))
    b = jnp.sqrt(-jnp.tanh(log_a) * (a * a + 1.0)) * (i * xc)

    hl, ap = [], []
    h = jnp.zeros((SUBLANES, bw), F32)
    prod = jnp.ones((SUBLANES, bw), F32)
    for g in range(seg):
        ag = a[g * SUBLANES:(g + 1) * SUBLANES]
        h = ag * h + b[g * SUBLANES:(g + 1) * SUBLANES]
        prod = ag * prod
        hl.append(h)
        ap.append(prod)
    c = h_ref[...]
    carry = []
    for s in range(SUBLANES):
        carry.append(c)
        c = prod[s:s + 1] * c + h[s:s + 1]
    h_ref[...] = c
    carry = jnp.concatenate(carry, axis=0)
    for g in range(seg):
        put(hs_ref, g * SUBLANES, hl[g] + ap[g] * carry)
    rows = []
    for n in range(ts // SUBLANES):
        start = ((n * SUBLANES) % seg) * SUBLANES + (n * SUBLANES) // seg
        rows.append(take(hs_ref, start, SUBLANES))
    hs = jnp.concatenate(rows, axis=0)
    y = jax.nn.gelu(y_ref[...].astype(F32), approximate=True)
    o_ref[...] = (y * hs).astype(o_ref.dtype)


def _rglru(proj, conv_w, conv_b, wg, bg, lam, batch, seq, ts=1024):
    t = proj.shape[0]
    width = proj.shape[1] // 2
    bw = width // LRU_BLOCKS
    ts = _tile(seq, ts)
    ns = seq // ts
    return pl.pallas_call(
        functools.partial(_rglru_kernel, ts=ts, bw=bw),
        out_shape=jax.ShapeDtypeStruct((t, width), BF16),
        grid=(batch, LRU_BLOCKS, ns),
        in_specs=[pl.BlockSpec((ts, bw), lambda b, n, s: (b * ns + s, n)),
                  pl.BlockSpec((ts, bw), lambda b, n, s: (b * ns + s, LRU_BLOCKS + n)),
                  pl.BlockSpec((CONV_WIDTH, bw), lambda b, n, s: (0, n)),
                  pl.BlockSpec((1, bw), lambda b, n, s: (0, n)),
                  pl.BlockSpec((None, bw, 2 * bw), lambda b, n, s: (n, 0, 0)),
                  pl.BlockSpec((None, 1, 2 * bw), lambda b, n, s: (n, 0, 0)),
                  pl.BlockSpec((1, bw), lambda b, n, s: (0, n))],
        out_specs=pl.BlockSpec((ts, bw), lambda b, n, s: (b * ns + s, n)),
        scratch_shapes=[pltpu.VMEM((SUBLANES, bw), F32), pltpu.VMEM((1, bw), F32),
                        pltpu.VMEM((bw // LANES, ts + SUBLANES * SUBLANES, LANES), F32),
                        pltpu.VMEM((bw // LANES, ts, LANES), F32)],
        compiler_params=_params("parallel", "parallel", "arbitrary"),
        name="rglru",
    )(proj, proj, conv_w, conv_b, wg, bg, lam)


def _rot_half_cols(w):
    half = w.shape[-1] // 2
    return jnp.concatenate([-w[..., half:], w[..., :half]], axis=-1)


def _prep_attn_w_in(w):
    k, n = w.shape
    qa_cols = A_HEADS * 2 * A_QK_DIM
    col_scale = jnp.where(jnp.arange(n) < qa_cols, A_QK_DIM ** -0.5 * LOG2E, 1.0).astype(w.dtype)
    kr = w[:, C_KR:]
    return jnp.concatenate(
        [(w * col_scale).astype(BF16), _rot_half_cols(kr).astype(BF16),
         jnp.zeros((k, C_PAD - C_END), BF16)], axis=1)


def _prep_w_uq(w):
    k = w.shape[0]
    w = w.reshape(k, B_HEADS, B_QK_DIM)
    rope = w[:, :, B_NOPE_DIM:]
    return jnp.concatenate([w[:, :, :B_NOPE_DIM], rope, _rot_half_cols(rope)],
                           axis=-1).reshape(k, B_HEADS * 2 * LANES).astype(BF16)


def _prep_w_ukv(w):
    k = w.shape[0]
    w = w.reshape(k, B_HEADS, B_NOPE_DIM + B_V_DIM)
    return jnp.concatenate([w[:, :, :B_NOPE_DIM].reshape(k, -1),
                            w[:, :, B_NOPE_DIM:].reshape(k, -1)], axis=1).astype(BF16)


def kernel(x, positions, norm_mix, norm_mlp, norm_final, attn_w_in, attn_lambda_q1,
           attn_lambda_k1, attn_lambda_q2, attn_lambda_k2, attn_subln, attn_q_norm,
           attn_kv_norm, attn_w_uq, attn_w_ukv, attn_w_out, rec_w_in, rec_conv_w,
           rec_conv_b, rec_w_a, rec_b_a, rec_w_x, rec_b_x, rec_lambda, rec_w_out,
           mlp_w1, mlp_w2):
    batch, seq, d = x.shape
    t = batch * seq
    h0 = x.reshape(t, d)
    row = lambda v: v.reshape(1, -1)

    half = B_ROPE_DIM // 2
    inv_freq = 1.0 / (ROPE_THETA ** (jnp.arange(0, B_ROPE_DIM, 2, dtype=F32) / B_ROPE_DIM))
    invf = jnp.tile(inv_freq, LANES // half).reshape(1, LANES)
    pos_lanes = jnp.broadcast_to(positions.astype(F32).reshape(t, 1), (t, LANES))
    cs = _rope_table(pos_lanes, invf)

    proj = _norm_matmul(h0, row(norm_mix[0]), _prep_attn_w_in(attn_w_in[0])[None], 0, BF16,
                        "attn_in_proj", tm=512, tn=C_PAD)
    qb, kvb = _mla_prep(proj, cs, row(attn_q_norm[0]), row(attn_kv_norm[0]),
                        _prep_w_uq(attn_w_uq[0]), _prep_w_ukv(attn_w_ukv[0]))
    lambda_init = 0.8 - 0.6 * math.exp(-0.3 * 0)
    lamv = jnp.stack([attn_lambda_q1[0], attn_lambda_k1[0],
                      attn_lambda_q2[0], attn_lambda_k2[0]]).astype(F32)
    oa = _diff_attn(proj, lamv, row(attn_subln[0]), batch, seq, lambda_init)
    ob = _mla_attn(qb, kvb, batch, seq)
    h1 = _matmul_res([oa, ob], attn_w_out.astype(BF16), 0, h0, "attn_out_proj")
    h2, w1_next, w2_next, rec_in_w, rec_out_w = _mlp(
        h1, row(norm_mlp[0]), mlp_w1[:1].astype(BF16), mlp_w2[:1].astype(BF16), 0,
        row(norm_final), False, "mlp0",
        cast=((mlp_w1, 1), (mlp_w2, 1), (rec_w_in, 0), (rec_w_out, 0)))

    proj2 = _norm_matmul(h2, row(norm_mix[1]), rec_in_w[None], 0, BF16,
                         "rec_in_proj", tn=2048)
    wg = jnp.concatenate([rec_w_a[0], rec_w_x[0]], axis=-1).astype(BF16)
    bg = jnp.concatenate([rec_b_a[0], rec_b_x[0]], axis=-1)[:, None, :]
    g = _rglru(proj2, rec_conv_w[0], row(rec_conv_b[0]), wg, bg, row(rec_lambda[0]),
               batch, seq)
    h3 = _matmul_res([g], rec_out_w[None], 0, h2, "rec_out_proj")
    out = _mlp(h3, row(norm_mlp[1]), w1_next[None], w2_next[None], 0, row(norm_final), True,
               "mlp1")
    return out.reshape(batch, seq, d)
```

```python
import functools
import math

import jax
import jax.numpy as jnp
from jax import lax
from jax.experimental import pallas as pl
from jax.experimental.pallas import tpu as pltpu

F32 = jnp.float32
BF16 = jnp.bfloat16

NORM_EPS = 1e-6
NEG_INF = -1e30
A_HEADS = 8
A_QK_DIM = 64
A_V_DIM = 128
A_SUBLN_EPS = 1e-5
B_HEADS = 8
B_Q_LORA = 768
B_KV_LORA = 512
B_NOPE_DIM = 128
B_ROPE_DIM = 64
B_V_DIM = 128
B_QK_DIM = B_NOPE_DIM + B_ROPE_DIM
ROPE_THETA = 10000.0
LRU_BLOCKS = 8
CONV_WIDTH = 4
LRU_C = 8.0
LOG2E = math.log2(math.e)

LANES = 128
SUBLANES = 8
VMEM_LIMIT = 56 * 1024 * 1024

MXU_N = 256
C_QA = 0
C_KA = C_QA + A_HEADS * 2 * A_QK_DIM
C_VA = C_KA + A_HEADS * 2 * A_QK_DIM
C_CQ = C_VA + A_HEADS * A_V_DIM
C_CKV = C_CQ + B_Q_LORA
C_KR = C_CKV + B_KV_LORA
C_END = C_KR + 2 * B_ROPE_DIM
C_PAD = -(-C_END // (3 * MXU_N)) * (3 * MXU_N)
CKV_BLK = 256


def _params(*sem):
    return pltpu.CompilerParams(dimension_semantics=sem, vmem_limit_bytes=VMEM_LIMIT)


def _tile(n, pref):
    t = min(n, pref)
    assert n % t == 0, (n, t)
    return t


def _cast_side_job(cast, steps, step_of):
    bf16_rows = 2 * SUBLANES
    in_specs, out_shape, out_specs = [], [], []
    for w, lyr in cast:
        _, r, c = w.shape
        rows = max(bf16_rows, r // steps)
        nblk = r // rows
        assert r % rows == 0 and steps % nblk == 0, (w.shape, steps)
        per = steps // nblk
        in_specs.append(pl.BlockSpec(
            (None, rows, c),
            functools.partial(lambda *ids, lyr, per: (lyr, step_of(*ids) // per, 0), lyr=lyr, per=per)))
        out_shape.append(jax.ShapeDtypeStruct((r, c), BF16))
        out_specs.append(pl.BlockSpec(
            (rows, c), functools.partial(lambda *ids, per: (step_of(*ids) // per, 0), per=per)))
    return in_specs, out_shape, out_specs


def _rms(x, g, eps):
    return x * lax.rsqrt(jnp.mean(x * x, axis=-1, keepdims=True) + eps) * g


def _rope_table_kernel(pos_ref, invf_ref, o_ref):
    ang = pos_ref[...] * invf_ref[...]
    lane = lax.broadcasted_iota(jnp.int32, ang.shape, 1)
    o_ref[...] = jnp.where(lane < B_ROPE_DIM, jnp.cos(ang), jnp.sin(ang))


def _rope_table(pos_lanes, invf):
    t = pos_lanes.shape[0]
    tm = _tile(t, 1024)
    return pl.pallas_call(
        _rope_table_kernel,
        out_shape=jax.ShapeDtypeStruct((t, LANES), F32),
        grid=(t // tm,),
        in_specs=[pl.BlockSpec((tm, LANES), lambda i: (i, 0)),
                  pl.BlockSpec((1, LANES), lambda i: (0, 0))],
        out_specs=pl.BlockSpec((tm, LANES), lambda i: (i, 0)),
        compiler_params=_params("parallel"),
        name="rope_table",
    )(pos_lanes, invf)


def _norm_matmul_kernel(*refs, n_cast):
    x_ref, g_ref, w_ref = refs[:3]
    o_ref = refs[3 + n_cast]
    xn_ref = refs[4 + 2 * n_cast]

    @pl.when(pl.program_id(1) == 0)
    def _():
        xn_ref[...] = _rms(x_ref[...], g_ref[...], NORM_EPS).astype(BF16)

    for src, dst in zip(refs[3:3 + n_cast], refs[4 + n_cast:4 + 2 * n_cast]):
        dst[...] = src[...].astype(BF16)

    o_ref[...] = jnp.dot(xn_ref[...], w_ref[...],
                         preferred_element_type=F32).astype(o_ref.dtype)


def _norm_matmul(x, g, w, layer, out_dtype, name, cast=(), tm=1024, tn=768):
    t, k = x.shape
    n = w.shape[2]
    tm = _tile(t, tm)
    tn = _tile(n, tn)
    nt, nj = t // tm, n // tn
    c_in, c_shape, c_out = _cast_side_job(cast, nt * nj, lambda i, j: i * nj + j)
    res = pl.pallas_call(
        functools.partial(_norm_matmul_kernel, n_cast=len(cast)),
        out_shape=(jax.ShapeDtypeStruct((t, n), out_dtype), *c_shape),
        grid=(nt, nj),
        in_specs=[pl.BlockSpec((tm, k), lambda i, j: (i, 0)),
                  pl.BlockSpec((1, k), lambda i, j: (0, 0)),
                  pl.BlockSpec((None, k, tn), lambda i, j: (layer, 0, j)), *c_in],
        out_specs=(pl.BlockSpec((tm, tn), lambda i, j: (i, j)), *c_out),
        scratch_shapes=[pltpu.VMEM((tm, k), BF16)],
        compiler_params=_params("arbitrary", "arbitrary"),
        name=name,
    )(x, g, w, *[a for a, _ in cast])
    return res if cast else res[0]


def _matmul_res_kernel(*refs, n_lhs, n_cast):
    lhs = refs[:n_lhs]
    ws = refs[n_lhs:2 * n_lhs]
    res_ref = refs[2 * n_lhs]
    cast_in = refs[2 * n_lhs + 1:2 * n_lhs + 1 + n_cast]
    o_ref = refs[2 * n_lhs + 1 + n_cast]
    for src, dst in zip(cast_in, refs[2 * n_lhs + 2 + n_cast:]):
        dst[...] = src[...].astype(BF16)
    acc = res_ref[...]
    for a, w in zip(lhs, ws):
        acc = acc + jnp.dot(a[...], w[...], preferred_element_type=F32)
    o_ref[...] = acc


def _matmul_res(lhs_list, w, layer, res, name, cast=(), tm=512, tn=2048):
    t, n = res.shape
    tm = _tile(t, tm)
    tn = _tile(n, tn)
    n_lhs = len(lhs_list)
    kp = lhs_list[0].shape[1]
    assert all(a.shape[1] == kp for a in lhs_list) and n_lhs * kp == w.shape[1]
    in_specs = []
    for a in lhs_list:
        in_specs.append(pl.BlockSpec((tm, kp), lambda i, j: (i, 0)))
    for part in range(n_lhs):
        in_specs.append(pl.BlockSpec(
            (None, kp, tn), functools.partial(lambda i, j, part: (layer, part, j), part=part)))
    in_specs.append(pl.BlockSpec((tm, tn), lambda i, j: (i, j)))
    nt, nj = t // tm, n // tn
    c_in, c_shape, c_out = _cast_side_job(cast, nt * nj, lambda i, j: i * nj + j)
    out = pl.pallas_call(
        functools.partial(_matmul_res_kernel, n_lhs=n_lhs, n_cast=len(cast)),
        out_shape=(jax.ShapeDtypeStruct((t, n), F32), *c_shape),
        grid=(nt, nj),
        in_specs=in_specs + c_in,
        out_specs=(pl.BlockSpec((tm, tn), lambda i, j: (i, j)), *c_out),
        compiler_params=_params("arbitrary", "arbitrary"),
        name=name,
    )(*lhs_list, *([w] * n_lhs), res, *[a for a, _ in cast])
    return out if cast else out[0]


def _mlp_kernel(*refs, n_cast, final_norm):
    h_ref, g_ref, w1_ref, w2_ref, gf_ref = refs[:5]
    cast_in = refs[5:5 + n_cast]
    o_ref = refs[5 + n_cast]
    cast_out = refs[6 + n_cast:6 + 2 * n_cast]
    xn_ref = refs[6 + 2 * n_cast]
    j = pl.program_id(1)

    @pl.when(j == 0)
    def _():
        h = h_ref[...]
        xn_ref[...] = _rms(h, g_ref[...], NORM_EPS).astype(BF16)
        o_ref[...] = h

    for src, dst in zip(cast_in, cast_out):
        dst[...] = src[...].astype(BF16)

    u = jnp.maximum(jnp.dot(xn_ref[...], w1_ref[...], preferred_element_type=F32), 0.0)
    u = (u * u).astype(BF16)
    o_ref[...] += jnp.dot(u, w2_ref[...], preferred_element_type=F32)

    if final_norm:
        @pl.when(j == pl.num_programs(1) - 1)
        def _():
            o_ref[...] = _rms(o_ref[...], gf_ref[...], NORM_EPS)


def _mlp(h, g, w1, w2, layer, gf, final_norm, name, cast=(), tm=512, tf=1024):
    t, d = h.shape
    f = w1.shape[2]
    tm = _tile(t, tm)
    tf = _tile(f, tf)
    nt, nj = t // tm, f // tf
    steps = nt * nj
    c_in, c_shape, c_out = _cast_side_job(cast, steps, lambda i, j: i * nj + j)
    in_specs = [pl.BlockSpec((tm, d), lambda i, j: (i, 0)),
                pl.BlockSpec((1, d), lambda i, j: (0, 0)),
                pl.BlockSpec((None, d, tf), lambda i, j: (layer, 0, j)),
                pl.BlockSpec((None, tf, d), lambda i, j: (layer, j, 0)),
                pl.BlockSpec((1, d), lambda i, j: (0, 0))] + c_in
    out_shape = [jax.ShapeDtypeStruct((t, d), F32)] + c_shape
    out_specs = [pl.BlockSpec((tm, d), lambda i, j: (i, 0))] + c_out
    res = pl.pallas_call(
        functools.partial(_mlp_kernel, n_cast=len(cast), final_norm=final_norm),
        out_shape=tuple(out_shape),
        grid=(nt, nj),
        in_specs=in_specs,
        out_specs=tuple(out_specs),
        scratch_shapes=[pltpu.VMEM((tm, d), BF16)],
        compiler_params=_params("arbitrary", "arbitrary"),
        name=name,
    )(h, g, w1, w2, gf, *[w for w, _ in cast])
    return res if cast else res[0]


def _mla_prep_kernel(cq_ref, ckv0_ref, ckv1_ref, kr_ref, cs_ref, gq_ref, gkv_ref, wq_ref,
                     wkv_ref, q_ref, kv_ref):
    cq = cq_ref[...].astype(F32)
    ckv = jnp.concatenate([ckv0_ref[...], ckv1_ref[...]], axis=1).astype(F32)
    cs = cs_ref[...]
    scale = B_QK_DIM ** -0.5 * LOG2E
    cqn = _rms(cq, gq_ref[...], NORM_EPS).astype(BF16)
    q = jnp.dot(cqn, wq_ref[...], preferred_element_type=F32)
    cs_scaled = cs * scale
    for h in range(B_HEADS):
        lo = h * 2 * LANES
        q_ref[:, lo:lo + LANES] = (q[:, lo:lo + LANES] * scale).astype(BF16)
        q_ref[:, lo + LANES:lo + 2 * LANES] = (
            q[:, lo + LANES:lo + 2 * LANES] * cs_scaled).astype(BF16)
    ckvn = _rms(ckv, gkv_ref[...], NORM_EPS).astype(BF16)
    kv = jnp.dot(ckvn, wkv_ref[...], preferred_element_type=F32)
    nk = B_HEADS * B_NOPE_DIM
    kv_ref[:, :nk] = kv[:, :nk].astype(BF16)
    kv_ref[:, nk + LANES:] = kv[:, nk:].astype(BF16)
    t = kr_ref[...].astype(F32) * cs
    kv_ref[:, nk:nk + LANES] = (t + pltpu.roll(t, B_ROPE_DIM, 1)).astype(BF16)


def _mla_prep(proj, cs, gq, gkv, wq, wkv, tm=512):
    t = proj.shape[0]
    tm = _tile(t, tm)
    assert C_CQ % B_Q_LORA == 0 and C_CKV % CKV_BLK == 0 and B_KV_LORA == 2 * CKV_BLK
    nq = B_HEADS * 2 * LANES
    nkv = B_HEADS * (B_NOPE_DIM + B_V_DIM) + LANES
    return pl.pallas_call(
        _mla_prep_kernel,
        out_shape=(jax.ShapeDtypeStruct((t, nq), BF16),
                   jax.ShapeDtypeStruct((t, nkv), BF16)),
        grid=(t // tm,),
        in_specs=[pl.BlockSpec((tm, B_Q_LORA), lambda i: (i, C_CQ // B_Q_LORA)),
                  pl.BlockSpec((tm, CKV_BLK), lambda i: (i, C_CKV // CKV_BLK)),
                  pl.BlockSpec((tm, CKV_BLK), lambda i: (i, C_CKV // CKV_BLK + 1)),
                  pl.BlockSpec((tm, LANES), lambda i: (i, C_KR // LANES)),
                  pl.BlockSpec((tm, LANES), lambda i: (i, 0)),
                  pl.BlockSpec((1, B_Q_LORA), lambda i: (0, 0)),
                  pl.BlockSpec((1, B_KV_LORA), lambda i: (0, 0)),
                  pl.BlockSpec(wq.shape, lambda i: (0, 0)),
                  pl.BlockSpec(wkv.shape, lambda i: (0, 0))],
        out_specs=(pl.BlockSpec((tm, nq), lambda i: (i, 0)),
                   pl.BlockSpec((tm, nkv), lambda i: (i, 0))),
        compiler_params=_params("parallel"),
        name="mla_prep",
    )(proj, proj, proj, proj, cs, gq, gkv, wq, wkv)


def _flash_sweep(q_at, k_at, v_at, finish, s_ref, p_ref, m_ref, al_ref, acc_ref, *,
                 nq, tq, rc):
    rows = m_ref.shape[0]
    tk = tq // 2
    nl = tk // LANES

    full = [(0, rows)]
    late = [(m * tq + tk, (m + 1) * tq) for m in range(rows // tq)]

    def qk(tile, kb, slot, spans):
        k = k_at(kb)
        for a, b in spans:
            s_ref[slot, a:b, :] = lax.dot_general(
                q_at(tile, a, b), k, (((1,), (1,)), ((), ())), preferred_element_type=F32)

    def pv(kb, slot, spans):
        v = v_at(kb)
        v1 = jnp.concatenate([v, jnp.ones_like(v)], axis=1)
        for a, b in spans:
            upd = jnp.dot(p_ref[slot, a:b, :], v1, preferred_element_type=F32)
            al = al_ref[slot, a:b, :]
            for j in range(2):
                ls = slice(j * LANES, (j + 1) * LANES)
                acc_ref[a:b, ls] = al * acc_ref[a:b, ls] + upd[:, ls]

    def softmax(s_slot, slot, spans, col0):
        def hidden(r0, j):
            return col0 is not None and col0 + j * LANES > r0 % tq + rc - 1

        def load(r0, j):
            x = s_ref[s_slot, r0:r0 + rc, j * LANES:(j + 1) * LANES]
            if col0 is not None and col0 + (j + 1) * LANES - 1 > r0 % tq:
                row = lax.broadcasted_iota(jnp.int32, x.shape, 0) + (r0 % tq)
                col = lax.broadcasted_iota(jnp.int32, x.shape, 1) + (col0 + j * LANES)
                x = jnp.where(col <= row, x, NEG_INF)
            return x

        chunks = [r0 for a, b in spans for r0 in range(a, b, rc)]
        for r0 in chunks:
            live = [j for j in range(nl) if not hidden(r0, j)]
            pm = load(r0, live[0])
            for j in live[1:]:
                pm = jnp.maximum(pm, load(r0, j))
            mb = jnp.broadcast_to(jnp.max(pm, axis=-1, keepdims=True), (rc, LANES))
            m_old = m_ref[r0:r0 + rc, :]
            m_new = jnp.maximum(m_old, mb)
            al_ref[slot, r0:r0 + rc, :] = jnp.exp2(m_old - m_new)
            m_ref[r0:r0 + rc, :] = m_new

        for r0 in chunks:
            m_new = m_ref[r0:r0 + rc, :]
            for j in range(nl):
                ls = slice(j * LANES, (j + 1) * LANES)
                if hidden(r0, j):
                    p_ref[slot, r0:r0 + rc, ls] = jnp.zeros((rc, LANES), BF16)
                else:
                    p_ref[slot, r0:r0 + rc, ls] = jnp.exp2(load(r0, j) - m_new).astype(BF16)

    qk(0, 0, 2, full)

    def tile(qi, carry):
        s_ref[0] = s_ref[2]
        m_ref[...] = jnp.full(m_ref.shape, NEG_INF, F32)
        acc_ref[...] = jnp.zeros(acc_ref.shape, F32)
        p_ref[1] = jnp.zeros(p_ref.shape[1:], BF16)
        al_ref[1] = jnp.ones(al_ref.shape[1:], F32)

        def pair(tp, c):
            e = 2 * tp
            qk(qi, e + 1, 1, full)
            softmax(0, 0, full, None)
            pv(jnp.maximum(e - 1, 0), 1, full)
            qk(qi, e + 2, 0, full)
            softmax(1, 1, full, None)
            pv(e, 0, full)
            return c

        lax.fori_loop(0, qi, pair, 0)
        e = 2 * qi
        qk(qi, e + 1, 1, late)
        softmax(0, 2, full, 0)
        pv(jnp.maximum(e - 1, 0), 1, full)
        qk(jnp.minimum(qi + 1, nq - 1), 0, 2, full)
        softmax(1, 3, late, tk)
        pv(e, 2, full)
        pv(e + 1, 3, late)
        finish(qi)
        return carry

    lax.fori_loop(0, nq, tile, 0)


def _flash_scratch(rows, tq, dv):
    tk = tq // 2
    return [pltpu.VMEM((3, rows, tk), F32),
            pltpu.VMEM((4, rows, tk), BF16),
            pltpu.VMEM((rows, LANES), F32),
            pltpu.VMEM((4, rows, LANES), F32),
            pltpu.VMEM((rows, 2 * dv), F32)]


def _diff_attn_kernel(q_ref, k_ref, v_ref, lam_ref, g_ref, o_ref, q_s, *scratch,
                      nq, tq, rc, lambda_init):
    rows = 2 * tq
    tk = tq // 2
    lane = lax.broadcasted_iota(jnp.int32, (tq, LANES), 1)
    zero = jnp.zeros((tq, LANES), BF16)
    for i in range(nq):
        q = q_ref[i * tq:(i + 1) * tq, :]
        q_s[i * rows:i * rows + tq, :] = jnp.where(lane < A_QK_DIM, q, zero)
        q_s[i * rows + tq:(i + 1) * rows, :] = jnp.where(lane < A_QK_DIM, zero, q)

    def q_at(tile, a, b):
        return q_s[pl.ds(pl.multiple_of(tile * rows + a, SUBLANES * 2), b - a), :]

    def k_at(kb):
        return k_ref[pl.ds(pl.multiple_of(kb * tk, tk), tk), :]

    def v_at(kb):
        return v_ref[pl.ds(pl.multiple_of(kb * tk, tk), tk), :]

    lv = lam_ref[...]
    lam = (jnp.exp(jnp.sum(lv[0:1] * lv[1:2], axis=-1, keepdims=True))
           - jnp.exp(jnp.sum(lv[2:3] * lv[3:4], axis=-1, keepdims=True))
           + lambda_init)
    acc_ref = scratch[-1]

    def finish(qi):
        acc = acc_ref[...]
        o = acc[:, :A_V_DIM] / acc[:, A_V_DIM:]
        d = o[:tq] - lam * o[tq:]
        o_ref[pl.ds(pl.multiple_of(qi * tq, tq), tq), :] = (
            _rms(d, g_ref[...], A_SUBLN_EPS) * (1.0 - lambda_init)).astype(o_ref.dtype)

    _flash_sweep(q_at, k_at, v_at, finish, *scratch, nq=nq, tq=tq, rc=rc)


def _diff_attn(proj, lamv, subln, batch, seq, lambda_init, tq=1024, rc=64):
    t = proj.shape[0]
    tq = _tile(seq, tq)
    nq = seq // tq
    qc, kc, vc = C_QA // LANES, C_KA // LANES, C_VA // LANES
    return pl.pallas_call(
        functools.partial(_diff_attn_kernel, nq=nq, tq=tq, rc=rc, lambda_init=lambda_init),
        out_shape=jax.ShapeDtypeStruct((t, A_HEADS * A_V_DIM), BF16),
        grid=(batch, A_HEADS),
        in_specs=[pl.BlockSpec((seq, LANES), lambda b, h: (b, qc + h)),
                  pl.BlockSpec((seq, LANES), lambda b, h: (b, kc + h)),
                  pl.BlockSpec((seq, LANES), lambda b, h: (b, vc + h)),
                  pl.BlockSpec(lamv.shape, lambda b, h: (0, 0)),
                  pl.BlockSpec((1, A_V_DIM), lambda b, h: (0, 0))],
        out_specs=pl.BlockSpec((seq, LANES), lambda b, h: (b, h)),
        scratch_shapes=[pltpu.VMEM((2 * seq, LANES), BF16)] + _flash_scratch(2 * tq, tq, A_V_DIM),
        compiler_params=_params("parallel", "parallel"),
        name="diff_attn",
    )(proj, proj, proj, lamv, subln)


def _mla_attn_kernel(q_ref, kn_ref, kr_ref, v_ref, o_ref, *scratch, nq, tq, rc):
    tk = tq // 2

    def q_at(tile, a, b):
        return q_ref[pl.ds(pl.multiple_of(tile * tq + a, SUBLANES * 2), b - a), :]

    def k_at(kb):
        sl = pl.ds(pl.multiple_of(kb * tk, tk), tk)
        return jnp.concatenate([kn_ref[sl, :], kr_ref[sl, :]], axis=1)

    def v_at(kb):
        return v_ref[pl.ds(pl.multiple_of(kb * tk, tk), tk), :]

    acc_ref = scratch[-1]

    def finish(qi):
        acc = acc_ref[...]
        o_ref[pl.ds(pl.multiple_of(qi * tq, tq), tq), :] = (
            acc[:, :B_V_DIM] / acc[:, B_V_DIM:]).astype(o_ref.dtype)

    _flash_sweep(q_at, k_at, v_at, finish, *scratch, nq=nq, tq=tq, rc=rc)


def _mla_attn(qb, kvb, batch, seq, tq=1024, rc=64):
    t = qb.shape[0]
    tq = _tile(seq, tq)
    nq = seq // tq
    return pl.pallas_call(
        functools.partial(_mla_attn_kernel, nq=nq, tq=tq, rc=rc),
        out_shape=jax.ShapeDtypeStruct((t, B_HEADS * B_V_DIM), BF16),
        grid=(batch, B_HEADS),
        in_specs=[pl.BlockSpec((seq, 2 * LANES), lambda b, h: (b, h)),
                  pl.BlockSpec((seq, LANES), lambda b, h: (b, h)),
                  pl.BlockSpec((seq, LANES), lambda b, h: (b, B_HEADS)),
                  pl.BlockSpec((seq, LANES), lambda b, h: (b, B_HEADS + 1 + h))],
        out_specs=pl.BlockSpec((seq, LANES), lambda b, h: (b, h)),
        scratch_shapes=_flash_scratch(tq, tq, B_V_DIM),
        compiler_params=_params("parallel", "parallel"),
        name="mla_attn",
    )(qb, kvb, kvb, kvb)


def _rglru_kernel(y_ref, x_ref, cw_ref, cb_ref, wg_ref, bg_ref, lam_ref, o_ref,
                  halo_ref, h_ref, xs_ref, hs_ref, *, ts, bw):
    seg = ts // SUBLANES
    pitch = xs_ref.shape[1] // SUBLANES
    nlc = bw // LANES

    def put(ref, r0, v):
        for c in range(nlc):
            ref[c, r0:r0 + v.shape[0], :] = v[:, c * LANES:(c + 1) * LANES]

    def take(ref, start, stride):
        return jnp.concatenate(
            [ref[c, pl.ds(start, SUBLANES, stride=stride), :] for c in range(nlc)], axis=1)

    @pl.when(pl.program_id(2) == 0)
    def _():
        halo_ref[...] = jnp.zeros_like(halo_ref)
        h_ref[...] = jnp.zeros_like(h_ref)

    x = x_ref[...].astype(F32)
    for s in range(SUBLANES):
        put(xs_ref, s * pitch, x[s * seg:(s + 1) * seg])
    xp = [take(xs_ref, g, pitch) for g in range(seg)]
    sub = lax.broadcasted_iota(jnp.int32, (SUBLANES, bw), 0)
    halo = halo_ref[...]
    before = {}
    for k in range(1, CONV_WIDTH):
        prev_tile = halo[SUBLANES - k:SUBLANES - k + 1]
        before[-k] = jnp.where(sub == 0, prev_tile, pltpu.roll(xp[seg - k], 1, 0))
    halo_ref[...] = x[ts - SUBLANES:]
    cw = cw_ref[...]
    cb = cb_ref[...]
    xc = []
    for g in range(seg):
        acc = cb + cw[CONV_WIDTH - 1:CONV_WIDTH] * xp[g]
        for k in range(1, CONV_WIDTH):
            src = xp[g - k] if g - k >= 0 else before[g - k]
            acc = acc + cw[CONV_WIDTH - 1 - k:CONV_WIDTH - k] * src
        xc.append(acc)
    xc = jnp.concatenate(xc, axis=0)

    gates = jnp.dot(xc.astype(BF16), wg_ref[...], preferred_element_type=F32) + bg_ref[...]
    r = jax.nn.sigmoid(gates[:, :bw])
    i = jax.nn.sigmoid(gates[:, bw:])
    decay = -LRU_C * jax.nn.softplus(-lam_ref[...])
    log_a = r * decay
    a = jnp.exp2(r * (decay * LOG2E))
    b = jnp.sqrt(-jnp.tanh(log_a) * (a * a + 1.0)) * (i * xc)

    hl, ap = [], []
    h = jnp.zeros((SUBLANES, bw), F32)
    prod = jnp.ones((SUBLANES, bw), F32)
    for g in range(seg):
        ag = a[g * SUBLANES:(g + 1) * SUBLANES]
        h = ag * h + b[g * SUBLANES:(g + 1) * SUBLANES]
        prod = ag * prod
        hl.append(h)
        ap.append(prod)
    c = h_ref[...]
    carry = []
    for s in range(SUBLANES):
        carry.append(c)
        c = prod[s:s + 1] * c + h[s:s + 1]
    h_ref[...] = c
    carry = jnp.concatenate(carry, axis=0)
    for g in range(seg):
        put(hs_ref, g * SUBLANES, hl[g] + ap[g] * carry)
    rows = []
    for n in range(ts // SUBLANES):
        start = ((n * SUBLANES) % seg) * SUBLANES + (n * SUBLANES) // seg
        rows.append(take(hs_ref, start, SUBLANES))
    hs = jnp.concatenate(rows, axis=0)
    y = jax.nn.gelu(y_ref[...].astype(F32), approximate=True)
    o_ref[...] = (y * hs).astype(o_ref.dtype)


def _rglru(proj, conv_w, conv_b, wg, bg, lam, batch, seq, ts=1024):
    t = proj.shape[0]
    width = proj.shape[1] // 2
    bw = width // LRU_BLOCKS
    ts = _tile(seq, ts)
    ns = seq // ts
    return pl.pallas_call(
        functools.partial(_rglru_kernel, ts=ts, bw=bw),
        out_shape=jax.ShapeDtypeStruct((t, width), BF16),
        grid=(batch, LRU_BLOCKS, ns),
        in_specs=[pl.BlockSpec((ts, bw), lambda b, n, s: (b * ns + s, n)),
                  pl.BlockSpec((ts, bw), lambda b, n, s: (b * ns + s, LRU_BLOCKS + n)),
                  pl.BlockSpec((CONV_WIDTH, bw), lambda b, n, s: (0, n)),
                  pl.BlockSpec((1, bw), lambda b, n, s: (0, n)),
                  pl.BlockSpec((None, bw, 2 * bw), lambda b, n, s: (n, 0, 0)),
                  pl.BlockSpec((None, 1, 2 * bw), lambda b, n, s: (n, 0, 0)),
                  pl.BlockSpec((1, bw), lambda b, n, s: (0, n))],
        out_specs=pl.BlockSpec((ts, bw), lambda b, n, s: (b * ns + s, n)),
        scratch_shapes=[pltpu.VMEM((SUBLANES, bw), F32), pltpu.VMEM((1, bw), F32),
                        pltpu.VMEM((bw // LANES, ts + SUBLANES * SUBLANES, LANES), F32),
                        pltpu.VMEM((bw // LANES, ts, LANES), F32)],
        compiler_params=_params("parallel", "parallel", "arbitrary"),
        name="rglru",
    )(proj, proj, conv_w, conv_b, wg, bg, lam)


def _rot_half_cols(w):
    half = w.shape[-1] // 2
    return jnp.concatenate([-w[..., half:], w[..., :half]], axis=-1)


def _prep_attn_w_in(w):
    k, n = w.shape
    qa_cols = A_HEADS * 2 * A_QK_DIM
    col_scale = jnp.where(jnp.arange(n) < qa_cols, A_QK_DIM ** -0.5 * LOG2E, 1.0).astype(w.dtype)
    kr = w[:, C_KR:]
    return jnp.concatenate(
        [(w * col_scale).astype(BF16), _rot_half_cols(kr).astype(BF16),
         jnp.zeros((k, C_PAD - C_END), BF16)], axis=1)


def _prep_w_uq(w):
    k = w.shape[0]
    w = w.reshape(k, B_HEADS, B_QK_DIM)
    rope = w[:, :, B_NOPE_DIM:]
    return jnp.concatenate([w[:, :, :B_NOPE_DIM], rope, _rot_half_cols(rope)],
                           axis=-1).reshape(k, B_HEADS * 2 * LANES).astype(BF16)


def _prep_w_ukv(w):
    k = w.shape[0]
    w = w.reshape(k, B_HEADS, B_NOPE_DIM + B_V_DIM)
    return jnp.concatenate([w[:, :, :B_NOPE_DIM].reshape(k, -1),
                            w[:, :, B_NOPE_DIM:].reshape(k, -1)], axis=1).astype(BF16)


def kernel(x, positions, norm_mix, norm_mlp, norm_final, attn_w_in, attn_lambda_q1,
           attn_lambda_k1, attn_lambda_q2, attn_lambda_k2, attn_subln, attn_q_norm,
           attn_kv_norm, attn_w_uq, attn_w_ukv, attn_w_out, rec_w_in, rec_conv_w,
           rec_conv_b, rec_w_a, rec_b_a, rec_w_x, rec_b_x, rec_lambda, rec_w_out,
           mlp_w1, mlp_w2):
    batch, seq, d = x.shape
    t = batch * seq
    h0 = x.reshape(t, d)
    row = lambda v: v.reshape(1, -1)

    half = B_ROPE_DIM // 2
    inv_freq = 1.0 / (ROPE_THETA ** (jnp.arange(0, B_ROPE_DIM, 2, dtype=F32) / B_ROPE_DIM))
    invf = jnp.tile(inv_freq, LANES // half).reshape(1, LANES)
    pos_lanes = jnp.broadcast_to(positions.astype(F32).reshape(t, 1), (t, LANES))
    cs = _rope_table(pos_lanes, invf)

    proj, w1_first = _norm_matmul(h0, row(norm_mix[0]), _prep_attn_w_in(attn_w_in[0])[None], 0,
                                  BF16, "attn_in_proj", cast=((mlp_w1, 0),), tm=512, tn=C_PAD)
    qb, kvb = _mla_prep(proj, cs, row(attn_q_norm[0]), row(attn_kv_norm[0]),
                        _prep_w_uq(attn_w_uq[0]), _prep_w_ukv(attn_w_ukv[0]))
    lambda_init = 0.8 - 0.6 * math.exp(-0.3 * 0)
    lamv = jnp.stack([attn_lambda_q1[0], attn_lambda_k1[0],
                      attn_lambda_q2[0], attn_lambda_k2[0]]).astype(F32)
    oa = _diff_attn(proj, lamv, row(attn_subln[0]), batch, seq, lambda_init)
    ob = _mla_attn(qb, kvb, batch, seq)
    h1, w2_first = _matmul_res([oa, ob], attn_w_out.astype(BF16), 0, h0, "attn_out_proj",
                               cast=((mlp_w2, 0),))
    h2, w1_next, w2_next, rec_in_w, rec_out_w = _mlp(
        h1, row(norm_mlp[0]), w1_first[None], w2_first[None], 0,
        row(norm_final), False, "mlp0",
        cast=((mlp_w1, 1), (mlp_w2, 1), (rec_w_in, 0), (rec_w_out, 0)))

    proj2 = _norm_matmul(h2, row(norm_mix[1]), rec_in_w[None], 0, BF16,
                         "rec_in_proj", tn=2048)
    wg = jnp.concatenate([rec_w_a[0], rec_w_x[0]], axis=-1).astype(BF16)
    bg = jnp.concatenate([rec_b_a[0], rec_b_x[0]], axis=-1)[:, None, :]
    g = _rglru(proj2, rec_conv_w[0], row(rec_conv_b[0]), wg, bg, row(rec_lambda[0]),
               batch, seq)
    h3 = _matmul_res([g], rec_out_w[None], 0, h2, "rec_out_proj")
    out = _mlp(h3, row(norm_mlp[1]), w1_next[None], w2_next[None], 0, row(norm_final), True,
               "mlp1")
    return out.reshape(batch, seq, d)
```
